```python
import jax, jax.numpy as jnp
from jax import lax
import numpy as np

D_MODEL = 1024
BATCH = 16
SEQ = 4096
DEPTH = 4

N_META = 16
EPS = 1e-6
D_FF = 11 * D_MODEL // 4
BRANCH_W = D_MODEL // 2
N_BRANCH = 3
CONV_W = BRANCH_W
CONV_K = 3
ATT_HEAD_DIM = 64
ATT_HEADS = BRANCH_W // ATT_HEAD_DIM
ATT_KV_HEADS = 2
ATT_W = ATT_HEADS * ATT_HEAD_DIM
IDX_HEADS = 4
IDX_HEAD_DIM = ATT_HEAD_DIM
IDX_TOPK_MAX = 256
Q_BLOCK = 128
MLSTM_HEADS = 4
MLSTM_HEAD_DIM = BRANCH_W // MLSTM_HEADS
MLSTM_W = MLSTM_HEADS * MLSTM_HEAD_DIM
MLSTM_CHUNK = 64
ROPE_THETA = 10000.0

SPLIT_SIZES = (CONV_W, CONV_W, CONV_W,
               ATT_W, ATT_KV_HEADS * ATT_HEAD_DIM, ATT_KV_HEADS * ATT_HEAD_DIM,
               IDX_HEADS * IDX_HEAD_DIM, IDX_HEAD_DIM, IDX_HEADS,
               MLSTM_W, MLSTM_W, MLSTM_W, MLSTM_W, MLSTM_HEADS, MLSTM_HEADS,
               N_BRANCH * D_MODEL)
IN_W = sum(SPLIT_SIZES)

kernel_name = 'hybrid_conv_dsa_mlstm_macaron_trunk'


def rmsnorm(x, g):
    xf = x.astype(jnp.float32)
    y = xf * lax.rsqrt(jnp.mean(xf * xf, axis=-1, keepdims=True) + EPS)
    return (y * g).astype(x.dtype)


def headwise_rmsnorm(h, g):
    hf = h.astype(jnp.float32)
    y = hf * lax.rsqrt(jnp.mean(hf * hf, axis=-1, keepdims=True) + EPS)
    return (y.reshape(h.shape[:2] + (-1,)) * g).astype(h.dtype)


def swiglu(u, w_gu, w_down):
    gate, up = jnp.split(u @ w_gu, 2, axis=-1)
    return (jax.nn.silu(gate) * up) @ w_down


def rope_tables(T, dim):
    inv = 1.0 / (ROPE_THETA ** (jnp.arange(0, dim, 2, dtype=jnp.float32) / dim))
    ang = jnp.arange(T, dtype=jnp.float32)[:, None] * inv[None, :]
    return jnp.cos(ang), jnp.sin(ang)


def rope(x, cos, sin):
    xf = x.astype(jnp.float32)
    x1, x2 = jnp.split(xf, 2, axis=-1)
    c = cos[None, :, None, :]
    s = sin[None, :, None, :]
    return jnp.concatenate([x1 * c - x2 * s, x1 * s + x2 * c], axis=-1).astype(x.dtype)


def causal_dwconv(u, w):
    T = u.shape[1]
    up = jnp.pad(u, ((0, 0), (CONV_K - 1, 0), (0, 0)))
    out = w[0] * up[:, 0:T]
    for j in range(1, CONV_K):
        out = out + w[j] * up[:, j:j + T]
    return out


def indexed_sparse_attention(q, k, v, qi, ki, wi, topk):
    f32 = jnp.float32
    B, T = q.shape[:2]
    nblk = -(-T // Q_BLOCK)
    pad = nblk * Q_BLOCK - T

    def blocks(a):
        a = jnp.pad(a, [(0, 0), (0, pad)] + [(0, 0)] * (a.ndim - 2))
        return jnp.moveaxis(a.reshape((B, nblk, Q_BLOCK) + a.shape[2:]), 1, 0)

    starts = jnp.arange(nblk, dtype=jnp.int32) * Q_BLOCK
    key_pos = jnp.arange(T, dtype=jnp.int32)
    kif = ki.astype(f32)
    gather = jax.vmap(lambda a, i: a[i])

    def one_block(args):
        qb, qib, wib, start = args
        q_pos = start + jnp.arange(Q_BLOCK, dtype=jnp.int32)
        causal = key_pos[None, :] <= q_pos[:, None]
        rel = jax.nn.relu(jnp.einsum('bqhd,bsd->bqhs', qib.astype(f32), kif) * IDX_HEAD_DIM ** -0.5)
        score = jnp.einsum('bqh,bqhs->bqs', wib.astype(f32) * IDX_HEADS ** -0.5, rel)
        score = jnp.where(causal[None], score, -jnp.inf)
        top_val, top_idx = lax.top_k(score, topk)
        valid = top_val > -jnp.inf
        k_sel = gather(k, top_idx)
        v_sel = gather(v, top_idx)
        qg = qb.reshape(B, Q_BLOCK, ATT_KV_HEADS, ATT_HEADS // ATT_KV_HEADS, ATT_HEAD_DIM)
        s = jnp.einsum('bqgrd,bqkgd->bqgrk', qg, k_sel).astype(f32) * ATT_HEAD_DIM ** -0.5
        s = jnp.where(valid[:, :, None, None, :], s, -jnp.inf)
        p = jax.nn.softmax(s, axis=-1).astype(v.dtype)
        o = jnp.einsum('bqgrk,bqkgd->bqgrd', p, v_sel)
        return o.reshape(B, Q_BLOCK, ATT_W)

    out = lax.map(one_block, (blocks(q), blocks(qi), blocks(wi), starts))
    return jnp.moveaxis(out, 0, 1).reshape(B, nblk * Q_BLOCK, ATT_W)[:, :T].astype(q.dtype)


def mlstm_chunkwise(q, k, v, i_pre, f_pre):
    f32 = jnp.float32
    B, T, H, Dh = q.shape
    L = MLSTM_CHUNK
    pad = (-N_META) % L

    def seq_chunks(a):
        a = jnp.pad(a.astype(f32), ((0, 0), (pad, 0), (0, 0), (0, 0)))
        return jnp.transpose(a.reshape(B, -1, L, H, Dh), (1, 0, 3, 2, 4))

    def gate_chunks(a, fill):
        a = jnp.pad(a, ((0, 0), (pad, 0), (0, 0)), constant_values=fill)
        return jnp.transpose(a.reshape(B, -1, L, H), (1, 0, 3, 2))

    qs = seq_chunks(q)
    ks = seq_chunks(k) * Dh ** -0.5
    vs = seq_chunks(v)
    logi = gate_chunks(i_pre.astype(f32), -jnp.inf)
    logf = gate_chunks(jax.nn.log_sigmoid(f_pre.astype(f32)), 0.0)
    tril = jnp.tril(jnp.ones((L, L), dtype=bool))

    def step(carry, xs):
        C, n, m = carry
        qc, kc, vc, ic, fc = xs
        b = jnp.cumsum(fc, axis=-1)
        bL = b[..., -1]
        Dm = jnp.where(tril, b[..., :, None] - b[..., None, :] + ic[..., None, :], -jnp.inf)
        a = b + m[..., None]
        mq = jnp.maximum(a, Dm.max(axis=-1))
        inter = jnp.exp(a - mq)
        wqk = jnp.exp(Dm - mq[..., None]) * jnp.einsum('bhld,bhsd->bhls', qc, kc)
        num = inter[..., None] * jnp.einsum('bhld,bhde->bhle', qc, C) + jnp.einsum('bhls,bhse->bhle', wqk, vc)
        den = inter * jnp.einsum('bhld,bhd->bhl', qc, n) + wqk.sum(axis=-1)
        hc = num / jnp.maximum(jnp.abs(den), jnp.exp(-mq))[..., None]
        g = bL[..., None] - b + ic
        m_new = jnp.maximum(bL + m, g.max(axis=-1))
        decay = jnp.exp(bL + m - m_new)
        wk = jnp.exp(g - m_new[..., None])[..., None] * kc
        C_new = decay[..., None, None] * C + jnp.einsum('bhld,bhle->bhde', wk, vc)
        n_new = decay[..., None] * n + wk.sum(axis=-2)
        return (C_new, n_new, m_new), hc

    init = (jnp.zeros((B, H, Dh, Dh), f32), jnp.zeros((B, H, Dh), f32), jnp.zeros((B, H), f32))
    _, hs = lax.scan(step, init, (qs, ks, vs, logi, logf))
    hs = jnp.transpose(hs, (1, 0, 3, 2, 4)).reshape(B, -1, H, Dh)[:, pad:]
    return hs.astype(q.dtype)


def hybrid_mixer(u, w_in, w_conv, b_igate, b_fgate, mh_g, w_branch, w_out, cos, sin, topk):
    B, T, _ = u.shape
    offs = [int(o) for o in np.cumsum(SPLIT_SIZES)[:-1]]
    (cx, cb, cc, q, k, v, qi, ki, wi, mq, mk, mv, mo, mi, mf, gt) = jnp.split(u @ w_in, offs, axis=-1)
    y_a = cb * causal_dwconv(cc * cx, w_conv)
    q = rope(q.reshape(B, T, ATT_HEADS, ATT_HEAD_DIM), cos, sin)
    k = rope(k.reshape(B, T, ATT_KV_HEADS, ATT_HEAD_DIM), cos, sin)
    v = v.reshape(B, T, ATT_KV_HEADS, ATT_HEAD_DIM)
    qi = rope(qi.reshape(B, T, IDX_HEADS, IDX_HEAD_DIM), cos, sin)
    ki = rope(ki[:, :, None, :], cos, sin)[:, :, 0, :]
    y_b = indexed_sparse_attention(q, k, v, qi, ki, wi, topk)
    hm = mlstm_chunkwise(mq.reshape(B, T, MLSTM_HEADS, MLSTM_HEAD_DIM),
                         mk.reshape(B, T, MLSTM_HEADS, MLSTM_HEAD_DIM),
                         mv.reshape(B, T, MLSTM_HEADS, MLSTM_HEAD_DIM),
                         mi + b_igate, mf + b_fgate)
    y_c = headwise_rmsnorm(hm, mh_g) * jax.nn.sigmoid(mo)
    gates = jax.nn.sigmoid(gt.reshape(B, T, N_BRANCH, D_MODEL))
    merged = (gates[:, :, 0] * (y_a @ w_branch[0])
              + gates[:, :, 1] * (y_b @ w_branch[1])
              + gates[:, :, 2] * (y_c @ w_branch[2]))
    return merged @ w_out


def setup_inputs(seed: int = 0) -> dict:
    key = jax.random.key(seed)
    ks = jax.random.split(key, 12)
    f32 = jnp.float32

    def nrm(k, shape, scale):
        return jax.random.normal(k, shape, f32) * scale

    x = nrm(ks[0], (BATCH, SEQ, D_MODEL), 1.0)
    meta = nrm(ks[1], (N_META, D_MODEL), 1.0)
    norm_g = 1.0 + nrm(ks[2], (DEPTH, 6, D_MODEL), 0.02)
    w_ffn_gu = nrm(ks[3], (DEPTH, 2, D_MODEL, 2 * D_FF), D_MODEL ** -0.5)
    w_ffn_down = nrm(ks[4], (DEPTH, 2, D_FF, D_MODEL), D_FF ** -0.5)
    w_in = nrm(ks[5], (DEPTH, D_MODEL, IN_W), D_MODEL ** -0.5)
    w_conv = nrm(ks[6], (DEPTH, CONV_K, CONV_W), CONV_K ** -0.5)
    b_igate = nrm(ks[7], (DEPTH, MLSTM_HEADS), 0.1)
    b_fgate = jnp.linspace(3.0, 6.0, MLSTM_HEADS, dtype=f32)[None, :] + nrm(ks[8], (DEPTH, MLSTM_HEADS), 0.1)
    mh_norm_g = 1.0 + nrm(ks[9], (DEPTH, MLSTM_W), 0.02)
    w_branch = nrm(ks[10], (DEPTH, N_BRANCH, BRANCH_W, D_MODEL), BRANCH_W ** -0.5)
    w_out = nrm(ks[11], (DEPTH, D_MODEL, D_MODEL), D_MODEL ** -0.5)
    return {'x': x, 'meta': meta, 'norm_g': norm_g, 'w_ffn_gu': w_ffn_gu, 'w_ffn_down': w_ffn_down,
            'w_in': w_in, 'w_conv': w_conv, 'b_igate': b_igate, 'b_fgate': b_fgate,
            'mh_norm_g': mh_norm_g, 'w_branch': w_branch, 'w_out': w_out}


def reference(x, meta, norm_g, w_ffn_gu, w_ffn_down, w_in, w_conv, b_igate, b_fgate, mh_norm_g, w_branch, w_out):
    B, S, _ = x.shape
    h = jnp.concatenate([jnp.broadcast_to(meta.astype(x.dtype)[None], (B, N_META, D_MODEL)), x], axis=1)
    T = h.shape[1]
    cos, sin = rope_tables(T, ATT_HEAD_DIM)
    topk = min(IDX_TOPK_MAX, S // 4)
    for l in range(DEPTH):
        g = norm_g[l]
        h = h + 0.5 * rmsnorm(swiglu(rmsnorm(h, g[0]), w_ffn_gu[l, 0], w_ffn_down[l, 0]), g[1])
        y = hybrid_mixer(rmsnorm(h, g[2]), w_in[l], w_conv[l], b_igate[l], b_fgate[l], mh_norm_g[l],
                         w_branch[l], w_out[l], cos, sin, topk)
        h = h + rmsnorm(y, g[3])
        h = h + 0.5 * rmsnorm(swiglu(rmsnorm(h, g[4]), w_ffn_gu[l, 1], w_ffn_down[l, 1]), g[5])
    return h[:, N_META:]
```

```python
import functools

import numpy as np
import jax
import jax.numpy as jnp
from jax import lax
from jax.experimental import pallas as pl
from jax.experimental.pallas import tpu as pltpu

F32 = jnp.float32
BF16 = jnp.bfloat16

D_MODEL = 1024
N_META = 16
EPS = 1e-6
D_FF = 11 * D_MODEL // 4
BRANCH_W = D_MODEL // 2
N_BRANCH = 3
CONV_K = 3
HEAD_DIM = 64
ATT_HEADS = 8
ATT_KV_HEADS = 2
ATT_REP = ATT_HEADS // ATT_KV_HEADS
IDX_HEADS = 4
IDX_TOPK_MAX = 256
ML_HEADS = 4
ML_DIM = 128
ROPE_THETA = 10000.0

LANE = 128
FF_CHUNK = 256
VMEM_LIMIT = 56 * 1024 * 1024
INT_MIN = -2 ** 31
NEG_BIG = -1e30

SEG_CONV = 3 * BRANCH_W
SEG_Q = ATT_HEADS * LANE
SEG_K = ATT_KV_HEADS * LANE
SEG_V = 2 * ATT_KV_HEADS * LANE
SEG_QI = IDX_HEADS * LANE
SEG_KI = LANE
SEG_SM = LANE
SEG_ML = 4 * BRANCH_W
OFF_CONV = 0
OFF_Q = OFF_CONV + SEG_CONV
OFF_QR = OFF_Q + SEG_Q
OFF_K = OFF_QR + SEG_Q
OFF_KR = OFF_K + SEG_K
OFF_V = OFF_KR + SEG_K
OFF_QI = OFF_V + SEG_V
OFF_QIR = OFF_QI + SEG_QI
OFF_KI = OFF_QIR + SEG_QI
OFF_KIR = OFF_KI + SEG_KI
OFF_SM = OFF_KIR + SEG_KI
OFF_ML = OFF_SM + SEG_SM
W_PACKED = OFF_ML + SEG_ML


def _pick_tile(n, candidates):
    for c in candidates:
        if n % c == 0:
            return c
    raise ValueError(f"no tile for {n}")


def _rms(x, g):
    return x * lax.rsqrt(jnp.mean(x * x, axis=-1, keepdims=True) + EPS) * g


def _dot(a, b):
    return jnp.dot(a, b, preferred_element_type=F32)


def _dot_nt(a, b):
    return lax.dot_general(a, b, (((1,), (1,)), ((), ())), preferred_element_type=F32)


def _params(*sem):
    return pltpu.CompilerParams(dimension_semantics=sem, vmem_limit_bytes=VMEM_LIMIT)


def _ffn_kernel(h_ref, gin_ref, gout_ref, wgu_ref, wd_ref, o_ref, act_ref):
    x = h_ref[...]
    xn = _rms(x, gin_ref[...]).astype(BF16)
    for c in range(D_FF // FF_CHUNK):
        lo = c * FF_CHUNK
        gate = _dot(xn, wgu_ref[:, lo:lo + FF_CHUNK])
        up = _dot(xn, wgu_ref[:, D_FF + lo:D_FF + lo + FF_CHUNK])
        act_ref[:, lo:lo + FF_CHUNK] = (gate * jax.nn.sigmoid(gate) * up).astype(BF16)
    y = _dot(act_ref[...], wd_ref[...])
    o_ref[...] = x + 0.5 * _rms(y, gout_ref[...])


def _ffn(h2, g_in, g_out, w_gu, w_down):
    rows = h2.shape[0]
    tm = _pick_tile(rows, (512, 384, 256, 128))
    const = lambda i: (0, 0)
    return pl.pallas_call(
        _ffn_kernel,
        grid=(rows // tm,),
        in_specs=[
            pl.BlockSpec((tm, D_MODEL), lambda i: (i, 0)),
            pl.BlockSpec((1, D_MODEL), const),
            pl.BlockSpec((1, D_MODEL), const),
            pl.BlockSpec((D_MODEL, 2 * D_FF), const, pipeline_mode=pl.Buffered(1)),
            pl.BlockSpec((D_FF, D_MODEL), const, pipeline_mode=pl.Buffered(1)),
        ],
        out_specs=pl.BlockSpec((tm, D_MODEL), lambda i: (i, 0)),
        out_shape=jax.ShapeDtypeStruct((rows, D_MODEL), F32),
        scratch_shapes=[pltpu.VMEM((tm, D_FF), BF16)],
        compiler_params=_params("arbitrary"),
        name="ffn",
    )(h2, g_in, g_out, w_gu, w_down)


def _inproj_kernel(h_ref, g_ref, cos_ref, sin_ref, wconv_ref, smb_ref, vone_ref, w_ref,
                   ya_ref, q_ref, k_ref, v_ref, qi_ref, ki_ref, sm_ref,
                   mq_ref, mk_ref, mv_ref, mo_ref, carry_ref):
    t = pl.program_id(1)
    tm = h_ref.shape[1]

    @pl.when(t == 0)
    def _():
        carry_ref[...] = jnp.zeros_like(carry_ref)

    xn = _rms(h_ref[0], g_ref[...]).astype(BF16)

    def proj(off, width):
        return _dot(xn, w_ref[:, off:off + width])

    cx = proj(OFF_CONV, BRANCH_W)
    cc = proj(OFF_CONV + 2 * BRANCH_W, BRANCH_W)
    z = cc * cx
    row = lax.broadcasted_iota(jnp.int32, z.shape, 0)
    prev1 = jnp.broadcast_to(carry_ref[7:8, :], z.shape)
    prev2 = jnp.broadcast_to(carry_ref[6:7, :], z.shape)
    z1 = jnp.where(row == 0, prev1, pltpu.roll(z, 1, 0))
    z2 = jnp.where(row == 0, prev2, jnp.where(row == 1, prev1, pltpu.roll(z, 2, 0)))
    carry_ref[...] = z[tm - 8:tm, :]
    conv = wconv_ref[0:1, :] * z2 + wconv_ref[1:2, :] * z1 + wconv_ref[2:3, :] * z
    cb = proj(OFF_CONV + BRANCH_W, BRANCH_W)
    ya_ref[0] = (cb * conv).astype(BF16)

    cos = cos_ref[...]
    sin = sin_ref[...]

    def rope_to(out_ref, off, off_rot, width, scale):
        for c in range(width // LANE):
            x = proj(off + c * LANE, LANE)
            xr = proj(off_rot + c * LANE, LANE)
            y = x * cos + xr * sin
            if scale != 1.0:
                y = y * scale
            out_ref[0, :, c * LANE:(c + 1) * LANE] = y.astype(BF16)

    rope_to(q_ref, OFF_Q, OFF_QR, SEG_Q, HEAD_DIM ** -0.5)
    rope_to(k_ref, OFF_K, OFF_KR, SEG_K, 1.0)
    rope_to(qi_ref, OFF_QI, OFF_QIR, SEG_QI, 1.0)
    rope_to(ki_ref, OFF_KI, OFF_KIR, SEG_KI, 1.0)
    v_ref[0] = (proj(OFF_V, SEG_V) + vone_ref[...]).astype(BF16)

    raw = proj(OFF_SM, SEG_SM) + smb_ref[...]
    col = lax.broadcasted_iota(jnp.int32, raw.shape, 1)
    logsig = jnp.minimum(raw, 0.0) - jnp.log(1.0 + jnp.exp(-jnp.abs(raw)))
    idx_scale = (IDX_HEADS ** -0.5) * (HEAD_DIM ** -0.5)
    sm_ref[0] = jnp.where(col < IDX_HEADS, raw * idx_scale,
                          jnp.where(col < 2 * IDX_HEADS, raw,
                                    jnp.where(col < 3 * IDX_HEADS, logsig, 0.0)))

    mq_ref[0] = proj(OFF_ML, BRANCH_W).astype(BF16)
    mk_ref[0] = (proj(OFF_ML + BRANCH_W, BRANCH_W) * (ML_DIM ** -0.5)).astype(BF16)
    mv_ref[0] = proj(OFF_ML + 2 * BRANCH_W, BRANCH_W).astype(BF16)
    mo_ref[0] = proj(OFF_ML + 3 * BRANCH_W, BRANCH_W)


def _inproj(h3, g, cos, sin, wconv, smb, vone, w_packed):
    B, TP, _ = h3.shape
    tm = _pick_tile(TP, (384, 256, 128))
    const = lambda b, t: (0, 0)
    tok = lambda b, t: (b, t, 0)

    def out(width, dtype):
        return (pl.BlockSpec((1, tm, width), tok), jax.ShapeDtypeStruct((B, TP, width), dtype))

    outs = [out(BRANCH_W, BF16), out(SEG_Q, BF16), out(SEG_K, BF16), out(SEG_V, BF16),
            out(SEG_QI, BF16), out(SEG_KI, BF16), out(SEG_SM, F32),
            out(BRANCH_W, BF16), out(BRANCH_W, BF16), out(BRANCH_W, BF16), out(BRANCH_W, F32)]
    return pl.pallas_call(
        _inproj_kernel,
        grid=(B, TP // tm),
        in_specs=[
            pl.BlockSpec((1, tm, D_MODEL), tok),
            pl.BlockSpec((1, D_MODEL), const),
            pl.BlockSpec((tm, LANE), lambda b, t: (t, 0)),
            pl.BlockSpec((tm, LANE), lambda b, t: (t, 0)),
            pl.BlockSpec((8, BRANCH_W), const),
            pl.BlockSpec((1, SEG_SM), const),
            pl.BlockSpec((1, SEG_V), const),
            pl.BlockSpec((D_MODEL, W_PACKED), const, pipeline_mode=pl.Buffered(1)),
        ],
        out_specs=[o[0] for o in outs],
        out_shape=[o[1] for o in outs],
        scratch_shapes=[pltpu.VMEM((8, BRANCH_W), F32)],
        compiler_params=_params("arbitrary", "arbitrary"),
        name="inproj",
    )(h3, g, cos, sin, wconv, smb, vone, w_packed)


def _score_key(sc):
    bits = lax.bitcast_convert_type(sc, jnp.int32)
    return jnp.where(bits < 0, INT_MIN - bits, bits)


def _attn_kernel(q_ref, qi_ref, sm_ref, k_ref, v_ref, ki_ref, o_ref,
                 keys_ref, wb_ref, q4_ref, qi4_ref, acc_ref, m_ref, *, topk):
    i = pl.program_id(1)
    blk = LANE
    row = lax.broadcasted_iota(jnp.int32, (blk, blk), 0)
    col = lax.broadcasted_iota(jnp.int32, (blk, blk), 1)

    def at(j):
        return pl.ds(pl.multiple_of(j * blk, blk), blk)

    for h in range(IDX_HEADS):
        qi4_ref[h * blk:(h + 1) * blk, :] = qi_ref[0, :, h * LANE:(h + 1) * LANE]
        wb_ref[h] = jnp.broadcast_to(sm_ref[0, :, h:h + 1], (blk, blk))

    def score_key(j):
        rel = _dot_nt(qi4_ref[...], ki_ref[0, at(j), :])
        sc = wb_ref[0] * jnp.maximum(rel[0:blk], 0.0)
        for h in range(1, IDX_HEADS):
            sc = sc + wb_ref[h] * jnp.maximum(rel[h * blk:(h + 1) * blk], 0.0)
        return _score_key(sc)

    def p1(j, c):
        keys_ref[:, at(j)] = score_key(j)
        return c

    lax.fori_loop(0, i, p1, 0)
    keys_ref[:, at(i)] = jnp.where(col <= row, score_key(i), INT_MIN)
    keys_ref[:, at(i + 1)] = jnp.full((blk, blk), INT_MIN, jnp.int32)

    nunits = (i + 2) // 2

    def count_where(pred):
        def body(u, acc):
            for half in range(2):
                j = 2 * u + half
                acc = acc + jnp.where(pred(keys_ref[:, at(j)], j), 1.0, 0.0)
            return acc
        acc = lax.fori_loop(0, nunits, body, jnp.zeros((blk, blk), F32))
        return jnp.sum(acc, axis=-1, keepdims=True)

    def count_ge(cand):
        cand_b = jnp.broadcast_to(cand, (blk, blk))
        return count_where(lambda kb, j: kb >= cand_b)

    kf = float(topk)
    zero = jnp.zeros((blk, 1), jnp.int32)
    r0 = jnp.where(count_ge(zero) >= kf, zero, INT_MIN)

    def bit_step(t, r):
        cand = r | jnp.left_shift(jnp.int32(1), 30 - t)
        return jnp.where(count_ge(cand) >= kf, cand, r)

    r = lax.fori_loop(0, 31, bit_step, r0)
    has_thr = r > INT_MIN
    c_ge = count_ge(r)
    tie = jnp.logical_and(has_thr, c_ge > kf)

    @pl.when(jnp.max(jnp.where(tie, 1.0, 0.0)) > 0.0)
    def _():
        r_b = jnp.broadcast_to(r, (blk, blk))
        need = kf - count_ge(r + 1)

        def pos_step(t, pos):
            cand = pos + jnp.left_shift(jnp.int32(1), 12 - t)
            cand_b = jnp.broadcast_to(cand, (blk, blk))
            cnt = count_where(lambda kb, j: jnp.logical_and(kb == r_b, col + j * blk < cand_b))
            return jnp.where(cnt < need, cand, pos)

        pos = lax.fori_loop(0, 13, pos_step, zero)
        pos_b = jnp.broadcast_to(pos, (blk, blk))
        tie_b = jnp.broadcast_to(tie, (blk, blk))

        def fix(j, c):
            kb = keys_ref[:, at(j)]
            drop = jnp.logical_and(jnp.logical_and(kb == r_b, col + j * blk > pos_b), tie_b)
            keys_ref[:, at(j)] = jnp.where(drop, r_b - 1, kb)
            return c

        lax.fori_loop(0, i + 1, fix, 0)

    thr_b = jnp.broadcast_to(jnp.maximum(r, INT_MIN + 1), (blk, blk))

    for g in range(ATT_KV_HEADS):
        for rr in range(ATT_REP):
            h = g * ATT_REP + rr
            q4_ref[g, rr * blk:(rr + 1) * blk, :] = q_ref[0, :, h * LANE:(h + 1) * LANE]
    m_ref[...] = jnp.full(m_ref.shape, NEG_BIG, F32)
    acc_ref[...] = jnp.zeros(acc_ref.shape, F32)

    def p3(j, c):
        bias = jnp.where(keys_ref[:, at(j)] >= thr_b, 0.0, NEG_BIG)
        for g in range(ATT_KV_HEADS):
            s4 = _dot_nt(q4_ref[g], k_ref[0, at(j), g * LANE:(g + 1) * LANE])
            for rr in range(ATT_REP):
                h = g * ATT_REP + rr
                s = s4[rr * blk:(rr + 1) * blk] + bias
                m_old = m_ref[h]
                m_new = jnp.maximum(m_old, jnp.max(s, axis=-1, keepdims=True))
                p = jnp.exp(s - m_new).astype(BF16)
                vv = v_ref[0, at(j), (2 * g + rr % 2) * LANE:(2 * g + rr % 2 + 1) * LANE]
                acc_ref[h] = jnp.exp(m_old - m_new) * acc_ref[h] + _dot(p, vv)
                m_ref[h] = m_new
        return c

    lax.fori_loop(0, i + 1, p3, 0)

    for pr in range(ATT_HEADS // 2):
        ae = acc_ref[2 * pr]
        ao = acc_ref[2 * pr + 1]
        oe = ae / ae[:, HEAD_DIM:HEAD_DIM + 1]
        oo = ao / ao[:, 0:1]
        o_ref[0, :, pr * LANE:(pr + 1) * LANE] = jnp.where(col < HEAD_DIM, oe, oo).astype(BF16)


def _attention(q, qi, sm, k, v, ki, topk):
    B, TP, _ = q.shape
    nblk = TP // LANE
    qmap = lambda b, i: (b, i, 0)
    kmap = lambda b, i: (b, 0, 0)
    return pl.pallas_call(
        functools.partial(_attn_kernel, topk=topk),
        grid=(B, nblk),
        in_specs=[
            pl.BlockSpec((1, LANE, SEG_Q), qmap),
            pl.BlockSpec((1, LANE, SEG_QI), qmap),
            pl.BlockSpec((1, LANE, SEG_SM), qmap),
            pl.BlockSpec((1, TP, SEG_K), kmap),
            pl.BlockSpec((1, TP, SEG_V), kmap),
            pl.BlockSpec((1, TP, SEG_KI), kmap),
        ],
        out_specs=pl.BlockSpec((1, LANE, BRANCH_W), qmap),
        out_shape=jax.ShapeDtypeStruct((B, TP, BRANCH_W), BF16),
        scratch_shapes=[
            pltpu.VMEM((LANE, TP + LANE), jnp.int32),
            pltpu.VMEM((IDX_HEADS, LANE, LANE), F32),
            pltpu.VMEM((ATT_KV_HEADS, ATT_REP * LANE, LANE), BF16),
            pltpu.VMEM((IDX_HEADS * LANE, LANE), BF16),
            pltpu.VMEM((ATT_HEADS, LANE, LANE), F32),
            pltpu.VMEM((ATT_HEADS, LANE, 1), F32),
        ],
        compiler_params=_params("arbitrary", "arbitrary"),
        name="attn",
    )(q, qi, sm, k, v, ki)


def _mlstm_kernel(q_ref, k_ref, v_ref, mo_ref, sm_ref, g_ref, o_ref, c_ref, m_ref):
    L = LANE

    @pl.when(pl.program_id(1) == 0)
    def _():
        c_ref[...] = jnp.zeros_like(c_ref)
        m_ref[...] = jnp.zeros_like(m_ref)

    row = lax.broadcasted_iota(jnp.int32, (L, L), 0)
    col = lax.broadcasted_iota(jnp.int32, (L, L), 1)
    tril = col <= row
    sm = sm_ref[0]
    cum = jnp.dot(jnp.where(tril, 1.0, 0.0), sm, precision=lax.Precision.HIGHEST,
                  preferred_element_type=F32)
    sm_t = sm.T
    cum_t = cum.T
    ones_col = jnp.where(col == 0, 1.0, 0.0).astype(BF16)

    for h in range(ML_HEADS):
        lanes = slice(h * ML_DIM, (h + 1) * ML_DIM)
        ci, cf = IDX_HEADS + h, 2 * IDX_HEADS + h
        b_col = cum[:, cf:cf + 1]
        b_row = cum_t[cf:cf + 1, :]
        b_last = cum[L - 1:L, cf:cf + 1]
        i_col = sm[:, ci:ci + 1]
        i_row = sm_t[ci:ci + 1, :]
        m_old = m_ref[h][0:1, 0:1]
        qh = q_ref[0, :, lanes]
        kh = k_ref[0, :, lanes]
        v_aug = jnp.concatenate([v_ref[0, :, lanes], ones_col], axis=1)

        dm = jnp.where(tril, b_col - b_row + i_row, -jnp.inf)
        a = b_col + m_old
        mq = jnp.maximum(a, jnp.max(dm, axis=-1, keepdims=True))
        inter = jnp.exp(a - mq)
        wqk = jnp.exp(dm - mq) * _dot_nt(qh, kh)
        c_old = c_ref[h]
        res = inter * _dot(qh, c_old.astype(BF16)) + _dot(wqk.astype(BF16), v_aug)
        den = res[:, ML_DIM:ML_DIM + 1]
        hc = res[:, 0:ML_DIM] / jnp.maximum(jnp.abs(den), jnp.exp(-mq))

        g_col = b_last - b_col + i_col
        m_new = jnp.maximum(b_last + m_old, jnp.max(g_col, axis=0, keepdims=True))
        decay = jnp.exp(b_last + m_old - m_new)
        wk = jnp.exp(g_col - m_new) * kh.astype(F32)
        c_ref[h] = decay * c_old + _dot(wk.T.astype(BF16), v_aug)
        m_ref[h] = jnp.broadcast_to(m_new, m_ref.shape[1:])

        hn = hc * lax.rsqrt(jnp.mean(hc * hc, axis=-1, keepdims=True) + EPS) * g_ref[:, lanes]
        o_ref[0, :, lanes] = (hn * jax.nn.sigmoid(mo_ref[0, :, lanes])).astype(BF16)


def _mlstm(mq, mk, mv, mo, sm, g):
    B, TP, _ = mq.shape
    tok = lambda b, c: (b, c, 0)
    return pl.pallas_call(
        _mlstm_kernel,
        grid=(B, TP // LANE),
        in_specs=[
            pl.BlockSpec((1, LANE, BRANCH_W), tok),
            pl.BlockSpec((1, LANE, BRANCH_W), tok),
            pl.BlockSpec((1, LANE, BRANCH_W), tok),
            pl.BlockSpec((1, LANE, BRANCH_W), tok),
            pl.BlockSpec((1, LANE, SEG_SM), tok),
            pl.BlockSpec((1, BRANCH_W), lambda b, c: (0, 0)),
        ],
        out_specs=pl.BlockSpec((1, LANE, BRANCH_W), tok),
        out_shape=jax.ShapeDtypeStruct((B, TP, BRANCH_W), BF16),
        scratch_shapes=[
            pltpu.VMEM((ML_HEADS, ML_DIM, 2 * ML_DIM), F32),
            pltpu.VMEM((ML_HEADS, 8, LANE), F32),
        ],
        compiler_params=_params("arbitrary", "arbitrary"),
        name="mlstm",
    )(mq, mk, mv, mo, sm, g)


def _merge_kernel(h_ref, ya_ref, yb_ref, yc_ref, gin_ref, gout_ref, wgt_ref, wbr_ref, wo_ref, o_ref):
    x = h_ref[...]
    xn = _rms(x, gin_ref[...]).astype(BF16)
    merged = None
    for n, y_ref in enumerate((ya_ref, yb_ref, yc_ref)):
        gate = jax.nn.sigmoid(_dot(xn, wgt_ref[:, n * D_MODEL:(n + 1) * D_MODEL]))
        term = gate * _dot(y_ref[...], wbr_ref[n])
        merged = term if merged is None else merged + term
    y = _dot(merged.astype(BF16), wo_ref[...])
    o_ref[...] = x + _rms(y, gout_ref[...])


def _merge(h2, ya, yb, yc, g_in, g_out, w_gt, w_br, w_o):
    rows = h2.shape[0]
    tm = _pick_tile(rows, (512, 384, 256, 128))
    const2 = lambda i: (0, 0)
    tok = lambda i: (i, 0)
    return pl.pallas_call(
        _merge_kernel,
        grid=(rows // tm,),
        in_specs=[
            pl.BlockSpec((tm, D_MODEL), tok),
            pl.BlockSpec((tm, BRANCH_W), tok),
            pl.BlockSpec((tm, BRANCH_W), tok),
            pl.BlockSpec((tm, BRANCH_W), tok),
            pl.BlockSpec((1, D_MODEL), const2),
            pl.BlockSpec((1, D_MODEL), const2),
            pl.BlockSpec((D_MODEL, N_BRANCH * D_MODEL), const2, pipeline_mode=pl.Buffered(1)),
            pl.BlockSpec((N_BRANCH, BRANCH_W, D_MODEL), lambda i: (0, 0, 0), pipeline_mode=pl.Buffered(1)),
            pl.BlockSpec((D_MODEL, D_MODEL), const2, pipeline_mode=pl.Buffered(1)),
        ],
        out_specs=pl.BlockSpec((tm, D_MODEL), tok),
        out_shape=jax.ShapeDtypeStruct((rows, D_MODEL), F32),
        compiler_params=_params("arbitrary"),
        name="merge",
    )(h2, ya, yb, yc, g_in, g_out, w_gt, w_br, w_o)


def _rot_cols(w):
    d, n = w.shape
    w = w.reshape(d, n // HEAD_DIM, 2, HEAD_DIM // 2)
    return jnp.stack([-w[:, :, 1], w[:, :, 0]], axis=2).reshape(d, n)


def _pad_heads(w):
    d, n = w.shape
    w = w.reshape(d, n // HEAD_DIM, HEAD_DIM)
    return jnp.pad(w, ((0, 0), (0, 0), (0, LANE - HEAD_DIM))).reshape(d, -1)


def _pack_w_in(w_in):
    sizes = (BRANCH_W, BRANCH_W, BRANCH_W, ATT_HEADS * HEAD_DIM, ATT_KV_HEADS * HEAD_DIM,
             ATT_KV_HEADS * HEAD_DIM, IDX_HEADS * HEAD_DIM, HEAD_DIM, IDX_HEADS,
             BRANCH_W, BRANCH_W, BRANCH_W, BRANCH_W, ML_HEADS, ML_HEADS, N_BRANCH * D_MODEL)
    offs = np.cumsum(sizes)[:-1].tolist()
    (cx, cb, cc, q, k, v, qi, ki, wi, mq, mk, mv, mo, mi, mf, gt) = jnp.split(w_in, offs, axis=1)
    d = w_in.shape[0]
    zeros64 = jnp.zeros((d, HEAD_DIM), w_in.dtype)
    v_parts = []
    for g in range(ATT_KV_HEADS):
        vg = v[:, g * HEAD_DIM:(g + 1) * HEAD_DIM]
        v_parts += [vg, zeros64, zeros64, vg]
    small = jnp.concatenate([wi, mi, mf, jnp.zeros((d, SEG_SM - 3 * IDX_HEADS), w_in.dtype)], axis=1)
    packed = jnp.concatenate([
        cx, cb, cc,
        _pad_heads(q), _pad_heads(_rot_cols(q)),
        _pad_heads(k), _pad_heads(_rot_cols(k)),
        *v_parts,
        _pad_heads(qi), _pad_heads(_rot_cols(qi)),
        _pad_heads(ki), _pad_heads(_rot_cols(ki)),
        small, mq, mk, mv, mo], axis=1)
    assert packed.shape[1] == W_PACKED
    return packed.astype(BF16), gt.astype(BF16)


def _rope_tables(tp):
    inv = 1.0 / (ROPE_THETA ** (jnp.arange(0, HEAD_DIM, 2, dtype=F32) / HEAD_DIM))
    ang = jnp.arange(tp, dtype=F32)[:, None] * inv[None, :]
    reps = 2 * LANE // HEAD_DIM
    return jnp.tile(jnp.cos(ang), (1, reps)), jnp.tile(jnp.sin(ang), (1, reps))


def kernel(x, meta, norm_g, w_ffn_gu, w_ffn_down, w_in, w_conv, b_igate, b_fgate, mh_norm_g, w_branch, w_out):
    B, S, _ = x.shape
    depth = norm_g.shape[0]
    T = N_META + S
    TP = -(-T // LANE) * LANE
    topk = min(IDX_TOPK_MAX, S // 4)

    h = jnp.concatenate([jnp.broadcast_to(meta.astype(x.dtype)[None], (B, N_META, D_MODEL)), x,
                         jnp.zeros((B, TP - T, D_MODEL), x.dtype)], axis=1)
    cos, sin = _rope_tables(TP)
    vone = np.zeros((1, SEG_V), np.float32)
    for g in range(ATT_KV_HEADS):
        vone[0, (2 * g) * LANE + HEAD_DIM] = 1.0
        vone[0, (2 * g + 1) * LANE] = 1.0
    vone = jnp.asarray(vone)

    h2 = h.reshape(B * TP, D_MODEL)
    for l in range(depth):
        g = norm_g[l][:, None, :]
        w_packed, w_gt = _pack_w_in(w_in[l])
        smb = jnp.concatenate([jnp.zeros((IDX_HEADS,), F32), b_igate[l], b_fgate[l],
                               jnp.zeros((SEG_SM - 3 * IDX_HEADS,), F32)])[None, :]
        wconv = jnp.pad(w_conv[l], ((0, 8 - CONV_K), (0, 0)))

        h2 = _ffn(h2, g[0], g[1], w_ffn_gu[l, 0].astype(BF16), w_ffn_down[l, 0].astype(BF16))
        (ya, q, k, v, qi, ki, sm, mq, mk, mv, mo) = _inproj(
            h2.reshape(B, TP, D_MODEL), g[2], cos, sin, wconv, smb, vone, w_packed)
        yb = _attention(q, qi, sm, k, v, ki, topk)
        yc = _mlstm(mq, mk, mv, mo, sm, mh_norm_g[l][None, :])
        flat = lambda a: a.reshape(B * TP, BRANCH_W)
        h2 = _merge(h2, flat(ya), flat(yb), flat(yc), g[2], g[3], w_gt,
                    w_branch[l].astype(BF16), w_out[l].astype(BF16))
        h2 = _ffn(h2, g[4], g[5], w_ffn_gu[l, 1].astype(BF16), w_ffn_down[l, 1].astype(BF16))
    return h2.reshape(B, TP, D_MODEL)[:, N_META:T]
```

```python
import functools

import numpy as np
import jax
import jax.numpy as jnp
from jax import lax
from jax.experimental import pallas as pl
from jax.experimental.pallas import tpu as pltpu

F32 = jnp.float32
BF16 = jnp.bfloat16

D_MODEL = 1024
N_META = 16
EPS = 1e-6
D_FF = 11 * D_MODEL // 4
BRANCH_W = D_MODEL // 2
N_BRANCH = 3
CONV_K = 3
HEAD_DIM = 64
ATT_HEADS = 8
ATT_KV_HEADS = 2
ATT_REP = ATT_HEADS // ATT_KV_HEADS
IDX_HEADS = 4
IDX_TOPK_MAX = 256
ML_HEADS = 4
ML_DIM = 128
ROPE_THETA = 10000.0

LANE = 128
FF_CHUNK = 256
VMEM_LIMIT = 56 * 1024 * 1024
INT_MIN = -2 ** 31
NEG_BIG = -1e30

SEG_CONV = 3 * BRANCH_W
SEG_Q = ATT_HEADS * LANE
SEG_K = ATT_KV_HEADS * LANE
SEG_V = 2 * ATT_KV_HEADS * LANE
SEG_QI = IDX_HEADS * LANE
SEG_KI = LANE
SEG_SM = LANE
SEG_ML = 4 * BRANCH_W
OFF_CONV = 0
OFF_Q = OFF_CONV + SEG_CONV
OFF_QR = OFF_Q + SEG_Q
OFF_K = OFF_QR + SEG_Q
OFF_KR = OFF_K + SEG_K
OFF_V = OFF_KR + SEG_K
OFF_QI = OFF_V + SEG_V
OFF_QIR = OFF_QI + SEG_QI
OFF_KI = OFF_QIR + SEG_QI
OFF_KIR = OFF_KI + SEG_KI
OFF_SM = OFF_KIR + SEG_KI
OFF_ML = OFF_SM + SEG_SM
W_PACKED = OFF_ML + SEG_ML


def _pick_tile(n, candidates):
    for c in candidates:
        if n % c == 0:
            return c
    raise ValueError(f"no tile for {n}")


def _rms(x, g):
    return x * lax.rsqrt(jnp.mean(x * x, axis=-1, keepdims=True) + EPS) * g


def _dot(a, b):
    return jnp.dot(a, b, preferred_element_type=F32)


def _dot_nt(a, b):
    return lax.dot_general(a, b, (((1,), (1,)), ((), ())), preferred_element_type=F32)


def _params(*sem):
    return pltpu.CompilerParams(dimension_semantics=sem, vmem_limit_bytes=VMEM_LIMIT)


def _ffn_kernel(h_ref, gin_ref, gout_ref, wgu_ref, wd_ref, o_ref, act_ref):
    x = h_ref[...]
    xn = _rms(x, gin_ref[...]).astype(BF16)
    for c in range(D_FF // FF_CHUNK):
        lo = c * FF_CHUNK
        gate = _dot(xn, wgu_ref[:, lo:lo + FF_CHUNK])
        up = _dot(xn, wgu_ref[:, D_FF + lo:D_FF + lo + FF_CHUNK])
        act_ref[:, lo:lo + FF_CHUNK] = (gate * jax.nn.sigmoid(gate) * up).astype(BF16)
    y = _dot(act_ref[...], wd_ref[...])
    o_ref[...] = x + 0.5 * _rms(y, gout_ref[...])


def _ffn(h2, g_in, g_out, w_gu, w_down):
    rows = h2.shape[0]
    tm = _pick_tile(rows, (512, 384, 256, 128))
    const = lambda i: (0, 0)
    return pl.pallas_call(
        _ffn_kernel,
        grid=(rows // tm,),
        in_specs=[
            pl.BlockSpec((tm, D_MODEL), lambda i: (i, 0)),
            pl.BlockSpec((1, D_MODEL), const),
            pl.BlockSpec((1, D_MODEL), const),
            pl.BlockSpec((D_MODEL, 2 * D_FF), const, pipeline_mode=pl.Buffered(1)),
            pl.BlockSpec((D_FF, D_MODEL), const, pipeline_mode=pl.Buffered(1)),
        ],
        out_specs=pl.BlockSpec((tm, D_MODEL), lambda i: (i, 0)),
        out_shape=jax.ShapeDtypeStruct((rows, D_MODEL), F32),
        scratch_shapes=[pltpu.VMEM((tm, D_FF), BF16)],
        compiler_params=_params("arbitrary"),
        name="ffn",
    )(h2, g_in, g_out, w_gu, w_down)


def _inproj_kernel(h_ref, g_ref, cos_ref, sin_ref, wconv_ref, smb_ref, vone_ref, w_ref,
                   ya_ref, q_ref, k_ref, v_ref, qi_ref, ki_ref, sm_ref,
                   mq_ref, mk_ref, mv_ref, mo_ref, carry_ref):
    t = pl.program_id(1)
    tm = h_ref.shape[1]

    @pl.when(t == 0)
    def _():
        carry_ref[...] = jnp.zeros_like(carry_ref)

    xn = _rms(h_ref[0], g_ref[...]).astype(BF16)

    def proj(off, width):
        return _dot(xn, w_ref[:, off:off + width])

    cx = proj(OFF_CONV, BRANCH_W)
    cc = proj(OFF_CONV + 2 * BRANCH_W, BRANCH_W)
    z = cc * cx
    row = lax.broadcasted_iota(jnp.int32, z.shape, 0)
    prev1 = jnp.broadcast_to(carry_ref[7:8, :], z.shape)
    prev2 = jnp.broadcast_to(carry_ref[6:7, :], z.shape)
    z1 = jnp.where(row == 0, prev1, pltpu.roll(z, 1, 0))
    z2 = jnp.where(row == 0, prev2, jnp.where(row == 1, prev1, pltpu.roll(z, 2, 0)))
    carry_ref[...] = z[tm - 8:tm, :]
    conv = wconv_ref[0:1, :] * z2 + wconv_ref[1:2, :] * z1 + wconv_ref[2:3, :] * z
    cb = proj(OFF_CONV + BRANCH_W, BRANCH_W)
    ya_ref[0] = (cb * conv).astype(BF16)

    cos = cos_ref[...]
    sin = sin_ref[...]

    def rope_to(out_ref, off, off_rot, width, scale):
        for c in range(width // LANE):
            x = proj(off + c * LANE, LANE)
            xr = proj(off_rot + c * LANE, LANE)
            y = x * cos + xr * sin
            if scale != 1.0:
                y = y * scale
            out_ref[0, :, c * LANE:(c + 1) * LANE] = y.astype(BF16)

    rope_to(q_ref, OFF_Q, OFF_QR, SEG_Q, HEAD_DIM ** -0.5)
    rope_to(k_ref, OFF_K, OFF_KR, SEG_K, 1.0)
    rope_to(qi_ref, OFF_QI, OFF_QIR, SEG_QI, 1.0)
    rope_to(ki_ref, OFF_KI, OFF_KIR, SEG_KI, 1.0)
    v_ref[0] = (proj(OFF_V, SEG_V) + vone_ref[...]).astype(BF16)

    raw = proj(OFF_SM, SEG_SM) + smb_ref[...]
    col = lax.broadcasted_iota(jnp.int32, raw.shape, 1)
    logsig = jnp.minimum(raw, 0.0) - jnp.log(1.0 + jnp.exp(-jnp.abs(raw)))
    idx_scale = (IDX_HEADS ** -0.5) * (HEAD_DIM ** -0.5)
    sm_ref[0] = jnp.where(col < IDX_HEADS, raw * idx_scale,
                          jnp.where(col < 2 * IDX_HEADS, raw,
                                    jnp.where(col < 3 * IDX_HEADS, logsig, 0.0)))

    mq_ref[0] = proj(OFF_ML, BRANCH_W).astype(BF16)
    mk_ref[0] = (proj(OFF_ML + BRANCH_W, BRANCH_W) * (ML_DIM ** -0.5)).astype(BF16)
    mv_ref[0] = proj(OFF_ML + 2 * BRANCH_W, BRANCH_W).astype(BF16)
    mo_ref[0] = proj(OFF_ML + 3 * BRANCH_W, BRANCH_W)


def _inproj(h3, g, cos, sin, wconv, smb, vone, w_packed):
    B, TP, _ = h3.shape
    tm = _pick_tile(TP, (384, 256, 128))
    const = lambda b, t: (0, 0)
    tok = lambda b, t: (b, t, 0)

    def out(width, dtype):
        return (pl.BlockSpec((1, tm, width), tok), jax.ShapeDtypeStruct((B, TP, width), dtype))

    outs = [out(BRANCH_W, BF16), out(SEG_Q, BF16), out(SEG_K, BF16), out(SEG_V, BF16),
            out(SEG_QI, BF16), out(SEG_KI, BF16), out(SEG_SM, F32),
            out(BRANCH_W, BF16), out(BRANCH_W, BF16), out(BRANCH_W, BF16), out(BRANCH_W, F32)]
    return pl.pallas_call(
        _inproj_kernel,
        grid=(B, TP // tm),
        in_specs=[
            pl.BlockSpec((1, tm, D_MODEL), tok),
            pl.BlockSpec((1, D_MODEL), const),
            pl.BlockSpec((tm, LANE), lambda b, t: (t, 0)),
            pl.BlockSpec((tm, LANE), lambda b, t: (t, 0)),
            pl.BlockSpec((8, BRANCH_W), const),
            pl.BlockSpec((1, SEG_SM), const),
            pl.BlockSpec((1, SEG_V), const),
            pl.BlockSpec((D_MODEL, W_PACKED), const, pipeline_mode=pl.Buffered(1)),
        ],
        out_specs=[o[0] for o in outs],
        out_shape=[o[1] for o in outs],
        scratch_shapes=[pltpu.VMEM((8, BRANCH_W), F32)],
        compiler_params=_params("arbitrary", "arbitrary"),
        name="inproj",
    )(h3, g, cos, sin, wconv, smb, vone, w_packed)


def _score_key(sc):
    bits = lax.bitcast_convert_type(sc, jnp.int32)
    return jnp.where(bits < 0, INT_MIN - bits, bits)


def _attn_kernel(q_ref, qi_ref, sm_ref, k_ref, v_ref, ki_ref, o_ref,
                 keys_ref, wb_ref, q4_ref, qi4_ref, acc_ref, m_ref, *, topk, ub):
    i = pl.program_id(1)
    blk = LANE
    uw = ub * blk
    nunits = (i + ub) // ub
    row = lax.broadcasted_iota(jnp.int32, (blk, blk), 0)
    col = lax.broadcasted_iota(jnp.int32, (blk, blk), 1)

    def unit(u):
        return pl.ds(pl.multiple_of(u * uw, blk), uw)

    def block(u, cb):
        return pl.ds(pl.multiple_of(u * uw + cb * blk, blk), blk)

    for h in range(IDX_HEADS):
        qi4_ref[h * blk:(h + 1) * blk, :] = qi_ref[0, :, h * LANE:(h + 1) * LANE]
        wb_ref[h] = jnp.broadcast_to(sm_ref[0, :, h:h + 1], (blk, blk))
    qpos = i * blk + row

    def p1(u, c):
        rel = _dot_nt(qi4_ref[...], ki_ref[0, unit(u), :])
        for cb in range(ub):
            lanes = slice(cb * blk, (cb + 1) * blk)
            sc = wb_ref[0] * jnp.maximum(rel[0:blk, lanes], 0.0)
            for h in range(1, IDX_HEADS):
                sc = sc + wb_ref[h] * jnp.maximum(rel[h * blk:(h + 1) * blk, lanes], 0.0)
            kpos = u * uw + cb * blk + col
            keys_ref[:, block(u, cb)] = jnp.where(kpos <= qpos, _score_key(sc), INT_MIN)
        return c

    lax.fori_loop(0, nunits, p1, 0)

    def count_where(pred):
        def body(u, acc):
            for cb in range(ub):
                acc = acc + jnp.where(pred(keys_ref[:, block(u, cb)], u * uw + cb * blk), 1.0, 0.0)
            return acc
        acc = lax.fori_loop(0, nunits, body, jnp.zeros((blk, blk), F32))
        return jnp.sum(acc, axis=-1, keepdims=True)

    def count_ge(cand):
        cand_b = jnp.broadcast_to(cand, (blk, blk))
        return count_where(lambda kb, base: kb >= cand_b)

    kf = float(topk)
    zero = jnp.zeros((blk, 1), jnp.int32)
    r0 = jnp.where(count_ge(zero) >= kf, zero, INT_MIN)

    def bit_step(t, r):
        cand = r | jnp.left_shift(jnp.int32(1), 30 - t)
        return jnp.where(count_ge(cand) >= kf, cand, r)

    r = lax.fori_loop(0, 31, bit_step, r0)
    has_thr = r > INT_MIN
    c_ge = count_ge(r)
    tie = jnp.logical_and(has_thr, c_ge > kf)

    @pl.when(jnp.max(jnp.where(tie, 1.0, 0.0)) > 0.0)
    def _():
        r_b = jnp.broadcast_to(r, (blk, blk))
        need = kf - count_ge(r + 1)

        def pos_step(t, pos):
            cand = pos + jnp.left_shift(jnp.int32(1), 12 - t)
            cand_b = jnp.broadcast_to(cand, (blk, blk))
            cnt = count_where(lambda kb, base: jnp.logical_and(kb == r_b, col + base < cand_b))
            return jnp.where(cnt < need, cand, pos)

        pos = lax.fori_loop(0, 13, pos_step, zero)
        pos_b = jnp.broadcast_to(pos, (blk, blk))
        tie_b = jnp.broadcast_to(tie, (blk, blk))

        def fix(u, c):
            for cb in range(ub):
                kb = keys_ref[:, block(u, cb)]
                late = col + (u * uw + cb * blk) > pos_b
                drop = jnp.logical_and(jnp.logical_and(kb == r_b, late), tie_b)
                keys_ref[:, block(u, cb)] = jnp.where(drop, r_b - 1, kb)
            return c

        lax.fori_loop(0, nunits, fix, 0)

    thr_b = jnp.broadcast_to(jnp.maximum(r, INT_MIN + 1), (blk, blk))

    for g in range(ATT_KV_HEADS):
        for rr in range(ATT_REP):
            h = g * ATT_REP + rr
            q4_ref[g, rr * blk:(rr + 1) * blk, :] = q_ref[0, :, h * LANE:(h + 1) * LANE]
    m_ref[...] = jnp.full(m_ref.shape, NEG_BIG, F32)
    acc_ref[...] = jnp.zeros(acc_ref.shape, F32)

    def p3(u, c):
        bias = [jnp.where(keys_ref[:, block(u, cb)] >= thr_b, 0.0, NEG_BIG) for cb in range(ub)]
        for g in range(ATT_KV_HEADS):
            s4 = _dot_nt(q4_ref[g], k_ref[0, unit(u), g * LANE:(g + 1) * LANE])
            for rr in range(ATT_REP):
                h = g * ATT_REP + rr
                s = [s4[rr * blk:(rr + 1) * blk, cb * blk:(cb + 1) * blk] + bias[cb] for cb in range(ub)]
                smax = s[0]
                for cb in range(1, ub):
                    smax = jnp.maximum(smax, s[cb])
                m_old = m_ref[h]
                m_new = jnp.maximum(m_old, jnp.max(smax, axis=-1, keepdims=True))
                p = jnp.concatenate([jnp.exp(sc - m_new).astype(BF16) for sc in s], axis=1)
                vv = v_ref[0, unit(u), (2 * g + rr % 2) * LANE:(2 * g + rr % 2 + 1) * LANE]
                acc_ref[h] = jnp.exp(m_old - m_new) * acc_ref[h] + _dot(p, vv)
                m_ref[h] = m_new
        return c

    lax.fori_loop(0, nunits, p3, 0)

    for pr in range(ATT_HEADS // 2):
        ae = acc_ref[2 * pr]
        ao = acc_ref[2 * pr + 1]
        oe = ae / ae[:, HEAD_DIM:HEAD_DIM + 1]
        oo = ao / ao[:, 0:1]
        o_ref[0, :, pr * LANE:(pr + 1) * LANE] = jnp.where(col < HEAD_DIM, oe, oo).astype(BF16)


def _attention(q, qi, sm, k, v, ki, topk):
    B, TP, _ = q.shape
    nblk = TP // LANE
    ub = _pick_tile(nblk, (3, 2, 1))
    assert TP <= 2 ** 13
    qmap = lambda b, i: (b, i, 0)
    kmap = lambda b, i: (b, 0, 0)
    return pl.pallas_call(
        functools.partial(_attn_kernel, topk=topk, ub=ub),
        grid=(B, nblk),
        in_specs=[
            pl.BlockSpec((1, LANE, SEG_Q), qmap),
            pl.BlockSpec((1, LANE, SEG_QI), qmap),
            pl.BlockSpec((1, LANE, SEG_SM), qmap),
            pl.BlockSpec((1, TP, SEG_K), kmap),
            pl.BlockSpec((1, TP, SEG_V), kmap),
            pl.BlockSpec((1, TP, SEG_KI), kmap),
        ],
        out_specs=pl.BlockSpec((1, LANE, BRANCH_W), qmap),
        out_shape=jax.ShapeDtypeStruct((B, TP, BRANCH_W), BF16),
        scratch_shapes=[
            pltpu.VMEM((LANE, TP), jnp.int32),
            pltpu.VMEM((IDX_HEADS, LANE, LANE), F32),
            pltpu.VMEM((ATT_KV_HEADS, ATT_REP * LANE, LANE), BF16),
            pltpu.VMEM((IDX_HEADS * LANE, LANE), BF16),
            pltpu.VMEM((ATT_HEADS, LANE, LANE), F32),
            pltpu.VMEM((ATT_HEADS, LANE, LANE), F32),
        ],
        compiler_params=_params("arbitrary", "arbitrary"),
        name="attn",
    )(q, qi, sm, k, v, ki)


def _mlstm_kernel(q_ref, k_ref, v_ref, mo_ref, sm_ref, g_ref, o_ref, c_ref, m_ref):
    L = LANE

    @pl.when(pl.program_id(1) == 0)
    def _():
        c_ref[...] = jnp.zeros_like(c_ref)
        m_ref[...] = jnp.zeros_like(m_ref)

    row = lax.broadcasted_iota(jnp.int32, (L, L), 0)
    col = lax.broadcasted_iota(jnp.int32, (L, L), 1)
    tril = col <= row
    sm = sm_ref[0]
    cum = jnp.dot(jnp.where(tril, 1.0, 0.0), sm, precision=lax.Precision.HIGHEST,
                  preferred_element_type=F32)
    sm_t = sm.T
    cum_t = cum.T
    ones_col = jnp.where(col == 0, 1.0, 0.0).astype(BF16)

    for h in range(ML_HEADS):
        lanes = slice(h * ML_DIM, (h + 1) * ML_DIM)
        ci, cf = IDX_HEADS + h, 2 * IDX_HEADS + h
        b_col = cum[:, cf:cf + 1]
        b_row = cum_t[cf:cf + 1, :]
        b_last = cum[L - 1:L, cf:cf + 1]
        i_col = sm[:, ci:ci + 1]
        i_row = sm_t[ci:ci + 1, :]
        m_old = m_ref[h][0:1, 0:1]
        qh = q_ref[0, :, lanes]
        kh = k_ref[0, :, lanes]
        v_aug = jnp.concatenate([v_ref[0, :, lanes], ones_col], axis=1)

        dm = jnp.where(tril, b_col - b_row + i_row, -jnp.inf)
        a = b_col + m_old
        mq = jnp.maximum(a, jnp.max(dm, axis=-1, keepdims=True))
        inter = jnp.exp(a - mq)
        wqk = jnp.exp(dm - mq) * _dot_nt(qh, kh)
        c_old = c_ref[h]
        res = inter * _dot(qh, c_old.astype(BF16)) + _dot(wqk.astype(BF16), v_aug)
        den = res[:, ML_DIM:ML_DIM + 1]
        hc = res[:, 0:ML_DIM] / jnp.maximum(jnp.abs(den), jnp.exp(-mq))

        g_col = b_last - b_col + i_col
        m_new = jnp.maximum(b_last + m_old, jnp.max(g_col, axis=0, keepdims=True))
        decay = jnp.exp(b_last + m_old - m_new)
        wk = jnp.exp(g_col - m_new) * kh.astype(F32)
        c_ref[h] = decay * c_old + _dot(wk.T.astype(BF16), v_aug)
        m_ref[h] = jnp.broadcast_to(m_new, m_ref.shape[1:])

        hn = hc * lax.rsqrt(jnp.mean(hc * hc, axis=-1, keepdims=True) + EPS) * g_ref[:, lanes]
        o_ref[0, :, lanes] = (hn * jax.nn.sigmoid(mo_ref[0, :, lanes])).astype(BF16)


def _mlstm(mq, mk, mv, mo, sm, g):
    B, TP, _ = mq.shape
    tok = lambda b, c: (b, c, 0)
    return pl.pallas_call(
        _mlstm_kernel,
        grid=(B, TP // LANE),
        in_specs=[
            pl.BlockSpec((1, LANE, BRANCH_W), tok),
            pl.BlockSpec((1, LANE, BRANCH_W), tok),
            pl.BlockSpec((1, LANE, BRANCH_W), tok),
            pl.BlockSpec((1, LANE, BRANCH_W), tok),
            pl.BlockSpec((1, LANE, SEG_SM), tok),
            pl.BlockSpec((1, BRANCH_W), lambda b, c: (0, 0)),
        ],
        out_specs=pl.BlockSpec((1, LANE, BRANCH_W), tok),
        out_shape=jax.ShapeDtypeStruct((B, TP, BRANCH_W), BF16),
        scratch_shapes=[
            pltpu.VMEM((ML_HEADS, ML_DIM, 2 * ML_DIM), F32),
            pltpu.VMEM((ML_HEADS, 8, LANE), F32),
        ],
        compiler_params=_params("arbitrary", "arbitrary"),
        name="mlstm",
    )(mq, mk, mv, mo, sm, g)


def _merge_kernel(h_ref, ya_ref, yb_ref, yc_ref, gin_ref, gout_ref, wgt_ref, wbr_ref, wo_ref, o_ref):
    x = h_ref[...]
    xn = _rms(x, gin_ref[...]).astype(BF16)
    merged = None
    for n, y_ref in enumerate((ya_ref, yb_ref, yc_ref)):
        gate = jax.nn.sigmoid(_dot(xn, wgt_ref[:, n * D_MODEL:(n + 1) * D_MODEL]))
        term = gate * _dot(y_ref[...], wbr_ref[n])
        merged = term if merged is None else merged + term
    y = _dot(merged.astype(BF16), wo_ref[...])
    o_ref[...] = x + _rms(y, gout_ref[...])


def _merge(h2, ya, yb, yc, g_in, g_out, w_gt, w_br, w_o):
    rows = h2.shape[0]
    tm = _pick_tile(rows, (512, 384, 256, 128))
    const2 = lambda i: (0, 0)
    tok = lambda i: (i, 0)
    return pl.pallas_call(
        _merge_kernel,
        grid=(rows // tm,),
        in_specs=[
            pl.BlockSpec((tm, D_MODEL), tok),
            pl.BlockSpec((tm, BRANCH_W), tok),
            pl.BlockSpec((tm, BRANCH_W), tok),
            pl.BlockSpec((tm, BRANCH_W), tok),
            pl.BlockSpec((1, D_MODEL), const2),
            pl.BlockSpec((1, D_MODEL), const2),
            pl.BlockSpec((D_MODEL, N_BRANCH * D_MODEL), const2, pipeline_mode=pl.Buffered(1)),
            pl.BlockSpec((N_BRANCH, BRANCH_W, D_MODEL), lambda i: (0, 0, 0), pipeline_mode=pl.Buffered(1)),
            pl.BlockSpec((D_MODEL, D_MODEL), const2, pipeline_mode=pl.Buffered(1)),
        ],
        out_specs=pl.BlockSpec((tm, D_MODEL), tok),
        out_shape=jax.ShapeDtypeStruct((rows, D_MODEL), F32),
        compiler_params=_params("arbitrary"),
        name="merge",
    )(h2, ya, yb, yc, g_in, g_out, w_gt, w_br, w_o)


def _rot_cols(w):
    d, n = w.shape
    w = w.reshape(d, n // HEAD_DIM, 2, HEAD_DIM // 2)
    return jnp.stack([-w[:, :, 1], w[:, :, 0]], axis=2).reshape(d, n)


def _pad_heads(w):
    d, n = w.shape
    w = w.reshape(d, n // HEAD_DIM, HEAD_DIM)
    return jnp.pad(w, ((0, 0), (0, 0), (0, LANE - HEAD_DIM))).reshape(d, -1)


def _pack_w_in(w_in):
    sizes = (BRANCH_W, BRANCH_W, BRANCH_W, ATT_HEADS * HEAD_DIM, ATT_KV_HEADS * HEAD_DIM,
             ATT_KV_HEADS * HEAD_DIM, IDX_HEADS * HEAD_DIM, HEAD_DIM, IDX_HEADS,
             BRANCH_W, BRANCH_W, BRANCH_W, BRANCH_W, ML_HEADS, ML_HEADS, N_BRANCH * D_MODEL)
    offs = np.cumsum(sizes)[:-1].tolist()
    (cx, cb, cc, q, k, v, qi, ki, wi, mq, mk, mv, mo, mi, mf, gt) = jnp.split(w_in, offs, axis=1)
    d = w_in.shape[0]
    zeros64 = jnp.zeros((d, HEAD_DIM), w_in.dtype)
    v_parts = []
    for g in range(ATT_KV_HEADS):
        vg = v[:, g * HEAD_DIM:(g + 1) * HEAD_DIM]
        v_parts += [vg, zeros64, zeros64, vg]
    small = jnp.concatenate([wi, mi, mf, jnp.zeros((d, SEG_SM - 3 * IDX_HEADS), w_in.dtype)], axis=1)
    packed = jnp.concatenate([
        cx, cb, cc,
        _pad_heads(q), _pad_heads(_rot_cols(q)),
        _pad_heads(k), _pad_heads(_rot_cols(k)),
        *v_parts,
        _pad_heads(qi), _pad_heads(_rot_cols(qi)),
        _pad_heads(ki), _pad_heads(_rot_cols(ki)),
        small, mq, mk, mv, mo], axis=1)
    assert packed.shape[1] == W_PACKED
    return packed.astype(BF16), gt.astype(BF16)


def _rope_tables(tp):
    inv = 1.0 / (ROPE_THETA ** (jnp.arange(0, HEAD_DIM, 2, dtype=F32) / HEAD_DIM))
    ang = jnp.arange(tp, dtype=F32)[:, None] * inv[None, :]
    reps = 2 * LANE // HEAD_DIM
    return jnp.tile(jnp.cos(ang), (1, reps)), jnp.tile(jnp.sin(ang), (1, reps))


def kernel(x, meta, norm_g, w_ffn_gu, w_ffn_down, w_in, w_conv, b_igate, b_fgate, mh_norm_g, w_branch, w_out):
    B, S, _ = x.shape
    depth = norm_g.shape[0]
    T = N_META + S
    TP = -(-T // LANE) * LANE
    topk = min(IDX_TOPK_MAX, S // 4)

    h = jnp.concatenate([jnp.broadcast_to(meta.astype(x.dtype)[None], (B, N_META, D_MODEL)), x,
                         jnp.zeros((B, TP - T, D_MODEL), x.dtype)], axis=1)
    cos, sin = _rope_tables(TP)
    vone = np.zeros((1, SEG_V), np.float32)
    for g in range(ATT_KV_HEADS):
        vone[0, (2 * g) * LANE + HEAD_DIM] = 1.0
        vone[0, (2 * g + 1) * LANE] = 1.0
    vone = jnp.asarray(vone)

    h2 = h.reshape(B * TP, D_MODEL)
    for l in range(depth):
        g = norm_g[l][:, None, :]
        w_packed, w_gt = _pack_w_in(w_in[l])
        smb = jnp.concatenate([jnp.zeros((IDX_HEADS,), F32), b_igate[l], b_fgate[l],
                               jnp.zeros((SEG_SM - 3 * IDX_HEADS,), F32)])[None, :]
        wconv = jnp.pad(w_conv[l], ((0, 8 - CONV_K), (0, 0)))

        h2 = _ffn(h2, g[0], g[1], w_ffn_gu[l, 0].astype(BF16), w_ffn_down[l, 0].astype(BF16))
        (ya, q, k, v, qi, ki, sm, mq, mk, mv, mo) = _inproj(
            h2.reshape(B, TP, D_MODEL), g[2], cos, sin, wconv, smb, vone, w_packed)
        yb = _attention(q, qi, sm, k, v, ki, topk)
        yc = _mlstm(mq, mk, mv, mo, sm, mh_norm_g[l][None, :])
        flat = lambda a: a.reshape(B * TP, BRANCH_W)
        h2 = _merge(h2, flat(ya), flat(yb), flat(yc), g[2], g[3], w_gt,
                    w_branch[l].astype(BF16), w_out[l].astype(BF16))
        h2 = _ffn(h2, g[4], g[5], w_ffn_gu[l, 1].astype(BF16), w_ffn_down[l, 1].astype(BF16))
    return h2.reshape(B, TP, D_MODEL)[:, N_META:T]
```

```python
import functools

import numpy as np
import jax
import jax.numpy as jnp
from jax import lax
from jax.experimental import pallas as pl
from jax.experimental.pallas import tpu as pltpu

F32 = jnp.float32
BF16 = jnp.bfloat16

D_MODEL = 1024
N_META = 16
EPS = 1e-6
D_FF = 11 * D_MODEL // 4
BRANCH_W = D_MODEL // 2
N_BRANCH = 3
CONV_K = 3
HEAD_DIM = 64
ATT_HEADS = 8
ATT_KV_HEADS = 2
ATT_REP = ATT_HEADS // ATT_KV_HEADS
IDX_HEADS = 4
IDX_TOPK_MAX = 256
ML_HEADS = 4
ML_DIM = 128
ROPE_THETA = 10000.0

LANE = 128
FF_CHUNK = 256
VMEM_LIMIT = 56 * 1024 * 1024
INT_MIN = -2 ** 31
NEG_BIG = -1e30

SEG_CONV = 3 * BRANCH_W
SEG_Q = ATT_HEADS * LANE
SEG_K = ATT_KV_HEADS * LANE
SEG_V = 2 * ATT_KV_HEADS * LANE
SEG_QI = IDX_HEADS * LANE
SEG_KI = LANE
SEG_SM = LANE
SEG_ML = 4 * BRANCH_W
OFF_CONV = 0
OFF_Q = OFF_CONV + SEG_CONV
OFF_QR = OFF_Q + SEG_Q
OFF_K = OFF_QR + SEG_Q
OFF_KR = OFF_K + SEG_K
OFF_V = OFF_KR + SEG_K
OFF_QI = OFF_V + SEG_V
OFF_QIR = OFF_QI + SEG_QI
OFF_KI = OFF_QIR + SEG_QI
OFF_KIR = OFF_KI + SEG_KI
OFF_SM = OFF_KIR + SEG_KI
OFF_ML = OFF_SM + SEG_SM
W_PACKED = OFF_ML + SEG_ML


def _pick_tile(n, candidates):
    for c in candidates:
        if n % c == 0:
            return c
    raise ValueError(f"no tile for {n}")


def _rms(x, g):
    return x * lax.rsqrt(jnp.mean(x * x, axis=-1, keepdims=True) + EPS) * g


def _dot(a, b):
    return jnp.dot(a, b, preferred_element_type=F32)


def _dot_nt(a, b):
    return lax.dot_general(a, b, (((1,), (1,)), ((), ())), preferred_element_type=F32)


def _params(*sem):
    return pltpu.CompilerParams(dimension_semantics=sem, vmem_limit_bytes=VMEM_LIMIT)


def _ffn_kernel(h_ref, gin_ref, gout_ref, wgu_ref, wd_ref, o_ref, act_ref):
    x = h_ref[...]
    xn = _rms(x, gin_ref[...]).astype(BF16)
    for c in range(D_FF // FF_CHUNK):
        lo = c * FF_CHUNK
        gate = _dot(xn, wgu_ref[:, lo:lo + FF_CHUNK])
        up = _dot(xn, wgu_ref[:, D_FF + lo:D_FF + lo + FF_CHUNK])
        act_ref[:, lo:lo + FF_CHUNK] = (gate * jax.nn.sigmoid(gate) * up).astype(BF16)
    y = _dot(act_ref[...], wd_ref[...])
    o_ref[...] = x + 0.5 * _rms(y, gout_ref[...])


def _ffn(h2, g_in, g_out, w_gu, w_down):
    rows = h2.shape[0]
    tm = _pick_tile(rows, (512, 384, 256, 128))
    const = lambda i: (0, 0)
    return pl.pallas_call(
        _ffn_kernel,
        grid=(rows // tm,),
        in_specs=[
            pl.BlockSpec((tm, D_MODEL), lambda i: (i, 0)),
            pl.BlockSpec((1, D_MODEL), const),
            pl.BlockSpec((1, D_MODEL), const),
            pl.BlockSpec((D_MODEL, 2 * D_FF), const, pipeline_mode=pl.Buffered(1)),
            pl.BlockSpec((D_FF, D_MODEL), const, pipeline_mode=pl.Buffered(1)),
        ],
        out_specs=pl.BlockSpec((tm, D_MODEL), lambda i: (i, 0)),
        out_shape=jax.ShapeDtypeStruct((rows, D_MODEL), F32),
        scratch_shapes=[pltpu.VMEM((tm, D_FF), BF16)],
        compiler_params=_params("arbitrary"),
        name="ffn",
    )(h2, g_in, g_out, w_gu, w_down)


def _inproj_kernel(h_ref, g_ref, cos_ref, sin_ref, wconv_ref, smb_ref, vone_ref, w_ref,
                   ya_ref, q_ref, k_ref, v_ref, qi_ref, ki_ref, sm_ref,
                   mq_ref, mk_ref, mv_ref, mo_ref, carry_ref):
    t = pl.program_id(1)
    tm = h_ref.shape[1]

    @pl.when(t == 0)
    def _():
        carry_ref[...] = jnp.zeros_like(carry_ref)

    xn = _rms(h_ref[0], g_ref[...]).astype(BF16)

    def proj(off, width):
        return _dot(xn, w_ref[:, off:off + width])

    cx = proj(OFF_CONV, BRANCH_W)
    cc = proj(OFF_CONV + 2 * BRANCH_W, BRANCH_W)
    z = cc * cx
    row = lax.broadcasted_iota(jnp.int32, z.shape, 0)
    prev1 = jnp.broadcast_to(carry_ref[7:8, :], z.shape)
    prev2 = jnp.broadcast_to(carry_ref[6:7, :], z.shape)
    z1 = jnp.where(row == 0, prev1, pltpu.roll(z, 1, 0))
    z2 = jnp.where(row == 0, prev2, jnp.where(row == 1, prev1, pltpu.roll(z, 2, 0)))
    carry_ref[...] = z[tm - 8:tm, :]
    conv = wconv_ref[0:1, :] * z2 + wconv_ref[1:2, :] * z1 + wconv_ref[2:3, :] * z
    cb = proj(OFF_CONV + BRANCH_W, BRANCH_W)
    ya_ref[0] = (cb * conv).astype(BF16)

    cos = cos_ref[...]
    sin = sin_ref[...]

    def rope_to(out_ref, off, off_rot, width, scale):
        for c in range(width // LANE):
            x = proj(off + c * LANE, LANE)
            xr = proj(off_rot + c * LANE, LANE)
            y = x * cos + xr * sin
            if scale != 1.0:
                y = y * scale
            out_ref[0, :, c * LANE:(c + 1) * LANE] = y.astype(BF16)

    rope_to(q_ref, OFF_Q, OFF_QR, SEG_Q, HEAD_DIM ** -0.5)
    rope_to(k_ref, OFF_K, OFF_KR, SEG_K, 1.0)
    rope_to(qi_ref, OFF_QI, OFF_QIR, SEG_QI, 1.0)
    rope_to(ki_ref, OFF_KI, OFF_KIR, SEG_KI, 1.0)
    v_ref[0] = (proj(OFF_V, SEG_V) + vone_ref[...]).astype(BF16)

    raw = proj(OFF_SM, SEG_SM) + smb_ref[...]
    col = lax.broadcasted_iota(jnp.int32, raw.shape, 1)
    logsig = jnp.minimum(raw, 0.0) - jnp.log(1.0 + jnp.exp(-jnp.abs(raw)))
    idx_scale = (IDX_HEADS ** -0.5) * (HEAD_DIM ** -0.5)
    sm_ref[0] = jnp.where(col < IDX_HEADS, raw * idx_scale,
                          jnp.where(col < 2 * IDX_HEADS, raw,
                                    jnp.where(col < 3 * IDX_HEADS, logsig, 0.0)))

    mq_ref[0] = proj(OFF_ML, BRANCH_W).astype(BF16)
    mk_ref[0] = (proj(OFF_ML + BRANCH_W, BRANCH_W) * (ML_DIM ** -0.5)).astype(BF16)
    mv_ref[0] = proj(OFF_ML + 2 * BRANCH_W, BRANCH_W).astype(BF16)
    mo_ref[0] = proj(OFF_ML + 3 * BRANCH_W, BRANCH_W)


def _inproj(h3, g, cos, sin, wconv, smb, vone, w_packed):
    B, TP, _ = h3.shape
    tm = _pick_tile(TP, (384, 256, 128))
    const = lambda b, t: (0, 0)
    tok = lambda b, t: (b, t, 0)

    def out(width, dtype):
        return (pl.BlockSpec((1, tm, width), tok), jax.ShapeDtypeStruct((B, TP, width), dtype))

    outs = [out(BRANCH_W, BF16), out(SEG_Q, BF16), out(SEG_K, BF16), out(SEG_V, BF16),
            out(SEG_QI, BF16), out(SEG_KI, BF16), out(SEG_SM, F32),
            out(BRANCH_W, BF16), out(BRANCH_W, BF16), out(BRANCH_W, BF16), out(BRANCH_W, F32)]
    return pl.pallas_call(
        _inproj_kernel,
        grid=(B, TP // tm),
        in_specs=[
            pl.BlockSpec((1, tm, D_MODEL), tok),
            pl.BlockSpec((1, D_MODEL), const),
            pl.BlockSpec((tm, LANE), lambda b, t: (t, 0)),
            pl.BlockSpec((tm, LANE), lambda b, t: (t, 0)),
            pl.BlockSpec((8, BRANCH_W), const),
            pl.BlockSpec((1, SEG_SM), const),
            pl.BlockSpec((1, SEG_V), const),
            pl.BlockSpec((D_MODEL, W_PACKED), const, pipeline_mode=pl.Buffered(1)),
        ],
        out_specs=[o[0] for o in outs],
        out_shape=[o[1] for o in outs],
        scratch_shapes=[pltpu.VMEM((8, BRANCH_W), F32)],
        compiler_params=_params("arbitrary", "arbitrary"),
        name="inproj",
    )(h3, g, cos, sin, wconv, smb, vone, w_packed)


def _score_key(sc):
    bits = lax.bitcast_convert_type(sc, jnp.int32)
    return jnp.where(bits < 0, INT_MIN - bits, bits)


def _attn_kernel(q_ref, qi_ref, sm_ref, k_ref, v_ref, ki_ref, o_ref,
                 keys_ref, wb_ref, acc_ref, m_ref, thr_ref, *, topk, ub):
    i = pl.program_id(1)
    blk = LANE
    uw = ub * blk
    nunits = i + 1
    row = lax.broadcasted_iota(jnp.int32, (uw, blk), 0)
    col = lax.broadcasted_iota(jnp.int32, (uw, blk), 1)

    def unit(u):
        return pl.ds(pl.multiple_of(u * uw, blk), uw)

    def block(u, cb):
        return pl.ds(pl.multiple_of(u * uw + cb * blk, blk), blk)

    for h in range(IDX_HEADS):
        wb_ref[h] = jnp.broadcast_to(sm_ref[0, :, h:h + 1], (uw, blk))
    qpos = i * uw + row

    def p1(u, c):
        rel = [_dot_nt(qi_ref[0, :, h * LANE:(h + 1) * LANE], ki_ref[0, unit(u), :])
               for h in range(IDX_HEADS)]
        for cb in range(ub):
            lanes = slice(cb * blk, (cb + 1) * blk)
            sc = wb_ref[0] * jnp.maximum(rel[0][:, lanes], 0.0)
            for h in range(1, IDX_HEADS):
                sc = sc + wb_ref[h] * jnp.maximum(rel[h][:, lanes], 0.0)
            kpos = u * uw + cb * blk + col
            keys_ref[:, block(u, cb)] = jnp.where(kpos <= qpos, _score_key(sc), INT_MIN)
        return c

    lax.fori_loop(0, nunits, p1, 0)

    rows_of = [slice(rb * blk, (rb + 1) * blk) for rb in range(ub)]
    col1 = lax.broadcasted_iota(jnp.int32, (blk, blk), 1)

    def lane_counts(rb, pred):
        def body(u, acc):
            for cb in range(ub):
                hit = pred(keys_ref[rows_of[rb], block(u, cb)], u * uw + cb * blk)
                acc = acc + jnp.where(hit, 1.0, 0.0)
            return acc
        return lax.fori_loop(0, nunits, body, jnp.zeros((blk, blk), F32))

    def count_where(rb, pred):
        return jnp.sum(lane_counts(rb, pred), axis=-1, keepdims=True)

    def ge(cand):
        cand_b = jnp.broadcast_to(cand, (blk, blk))
        return lambda kb, base: kb >= cand_b

    def count_ge(rb, cand):
        return count_where(rb, ge(cand))

    kf = float(topk)
    zero = jnp.zeros((blk, 1), jnp.int32)
    r0 = tuple(jnp.where(count_ge(rb, zero) >= kf, zero, INT_MIN) for rb in range(ub))

    def bit_step(t, rs):
        bit = jnp.left_shift(jnp.int32(1), 30 - t)
        cands = [rs[rb] | bit for rb in range(ub)]
        accs = [lane_counts(rb, ge(cands[rb])) for rb in range(ub)]
        return tuple(jnp.where(jnp.sum(accs[rb], axis=-1, keepdims=True) >= kf, cands[rb], rs[rb])
                     for rb in range(ub))

    rs = lax.fori_loop(0, 31, bit_step, r0)
    ties = [jnp.logical_and(rs[rb] > INT_MIN, count_ge(rb, rs[rb]) > kf) for rb in range(ub)]
    any_tie = jnp.max(jnp.where(ties[0], 1.0, 0.0))
    for rb in range(1, ub):
        any_tie = jnp.maximum(any_tie, jnp.max(jnp.where(ties[rb], 1.0, 0.0)))

    @pl.when(any_tie > 0.0)
    def _():
        for rb in range(ub):
            r_b = jnp.broadcast_to(rs[rb], (blk, blk))
            tie_b = jnp.broadcast_to(ties[rb], (blk, blk))
            need = kf - count_ge(rb, rs[rb] + 1)

            def pos_step(t, pos):
                cand = pos + jnp.left_shift(jnp.int32(1), 12 - t)
                cand_b = jnp.broadcast_to(cand, (blk, blk))
                cnt = count_where(rb, lambda kb, base: jnp.logical_and(kb == r_b, col1 + base < cand_b))
                return jnp.where(cnt < need, cand, pos)

            pos_b = jnp.broadcast_to(lax.fori_loop(0, 13, pos_step, zero), (blk, blk))

            def fix(u, c):
                for cb in range(ub):
                    kb = keys_ref[rows_of[rb], block(u, cb)]
                    late = col1 + (u * uw + cb * blk) > pos_b
                    drop = jnp.logical_and(jnp.logical_and(kb == r_b, late), tie_b)
                    keys_ref[rows_of[rb], block(u, cb)] = jnp.where(drop, r_b - 1, kb)
                return c

            lax.fori_loop(0, nunits, fix, 0)

    for rb in range(ub):
        thr_ref[rows_of[rb], :] = jnp.broadcast_to(jnp.maximum(rs[rb], INT_MIN + 1), (blk, blk))
    thr_b = thr_ref[...]

    m_ref[...] = jnp.full(m_ref.shape, NEG_BIG, F32)
    acc_ref[...] = jnp.zeros(acc_ref.shape, F32)

    def p3(u, c):
        bias = [jnp.where(keys_ref[:, block(u, cb)] >= thr_b, 0.0, NEG_BIG) for cb in range(ub)]
        for g in range(ATT_KV_HEADS):
            for rr in range(ATT_REP):
                h = g * ATT_REP + rr
                sh = _dot_nt(q_ref[0, :, h * LANE:(h + 1) * LANE],
                             k_ref[0, unit(u), g * LANE:(g + 1) * LANE])
                s = [sh[:, cb * blk:(cb + 1) * blk] + bias[cb] for cb in range(ub)]
                smax = s[0]
                for cb in range(1, ub):
                    smax = jnp.maximum(smax, s[cb])
                m_old = m_ref[h]
                m_new = jnp.maximum(m_old, jnp.max(smax, axis=-1, keepdims=True))
                p = jnp.concatenate([jnp.exp(sc - m_new).astype(BF16) for sc in s], axis=1)
                vv = v_ref[0, unit(u), (2 * g + rr % 2) * LANE:(2 * g + rr % 2 + 1) * LANE]
                acc_ref[h] = jnp.exp(m_old - m_new) * acc_ref[h] + _dot(p, vv)
                m_ref[h] = m_new
        return c

    lax.fori_loop(0, nunits, p3, 0)

    for pr in range(ATT_HEADS // 2):
        ae = acc_ref[2 * pr]
        ao = acc_ref[2 * pr + 1]
        oe = ae / ae[:, HEAD_DIM:HEAD_DIM + 1]
        oo = ao / ao[:, 0:1]
        o_ref[0, :, pr * LANE:(pr + 1) * LANE] = jnp.where(col < HEAD_DIM, oe, oo).astype(BF16)


def _attention(q, qi, sm, k, v, ki, topk):
    B, TP, _ = q.shape
    nblk = TP // LANE
    ub = _pick_tile(nblk, (3, 2, 1))
    uw = ub * LANE
    assert TP <= 2 ** 13
    qmap = lambda b, i: (b, i, 0)
    kmap = lambda b, i: (b, 0, 0)
    return pl.pallas_call(
        functools.partial(_attn_kernel, topk=topk, ub=ub),
        grid=(B, TP // uw),
        in_specs=[
            pl.BlockSpec((1, uw, SEG_Q), qmap),
            pl.BlockSpec((1, uw, SEG_QI), qmap),
            pl.BlockSpec((1, uw, SEG_SM), qmap),
            pl.BlockSpec((1, TP, SEG_K), kmap),
            pl.BlockSpec((1, TP, SEG_V), kmap),
            pl.BlockSpec((1, TP, SEG_KI), kmap),
        ],
        out_specs=pl.BlockSpec((1, uw, BRANCH_W), qmap),
        out_shape=jax.ShapeDtypeStruct((B, TP, BRANCH_W), BF16),
        scratch_shapes=[
            pltpu.VMEM((uw, TP), jnp.int32),
            pltpu.VMEM((IDX_HEADS, uw, LANE), F32),
            pltpu.VMEM((ATT_HEADS, uw, LANE), F32),
            pltpu.VMEM((ATT_HEADS, uw, LANE), F32),
            pltpu.VMEM((uw, LANE), jnp.int32),
        ],
        compiler_params=_params("arbitrary", "arbitrary"),
        name="attn",
    )(q, qi, sm, k, v, ki)


def _mlstm_kernel(q_ref, k_ref, v_ref, mo_ref, sm_ref, g_ref, o_ref, c_ref, m_ref):
    L = LANE

    @pl.when(pl.program_id(1) == 0)
    def _():
        c_ref[...] = jnp.zeros_like(c_ref)
        m_ref[...] = jnp.zeros_like(m_ref)

    row = lax.broadcasted_iota(jnp.int32, (L, L), 0)
    col = lax.broadcasted_iota(jnp.int32, (L, L), 1)
    tril = col <= row
    sm = sm_ref[0]
    cum = jnp.dot(jnp.where(tril, 1.0, 0.0), sm, precision=lax.Precision.HIGHEST,
                  preferred_element_type=F32)
    sm_t = sm.T
    cum_t = cum.T
    ones_col = jnp.where(col == 0, 1.0, 0.0).astype(BF16)

    for h in range(ML_HEADS):
        lanes = slice(h * ML_DIM, (h + 1) * ML_DIM)
        ci, cf = IDX_HEADS + h, 2 * IDX_HEADS + h
        b_col = cum[:, cf:cf + 1]
        b_row = cum_t[cf:cf + 1, :]
        b_last = cum[L - 1:L, cf:cf + 1]
        i_col = sm[:, ci:ci + 1]
        i_row = sm_t[ci:ci + 1, :]
        m_old = m_ref[h][0:1, 0:1]
        qh = q_ref[0, :, lanes]
        kh = k_ref[0, :, lanes]
        v_aug = jnp.concatenate([v_ref[0, :, lanes], ones_col], axis=1)

        dm = jnp.where(tril, b_col - b_row + i_row, -jnp.inf)
        a = b_col + m_old
        mq = jnp.maximum(a, jnp.max(dm, axis=-1, keepdims=True))
        inter = jnp.exp(a - mq)
        wqk = jnp.exp(dm - mq) * _dot_nt(qh, kh)
        c_old = c_ref[h]
        res = inter * _dot(qh, c_old.astype(BF16)) + _dot(wqk.astype(BF16), v_aug)
        den = res[:, ML_DIM:ML_DIM + 1]
        hc = res[:, 0:ML_DIM] / jnp.maximum(jnp.abs(den), jnp.exp(-mq))

        g_col = b_last - b_col + i_col
        m_new = jnp.maximum(b_last + m_old, jnp.max(g_col, axis=0, keepdims=True))
        decay = jnp.exp(b_last + m_old - m_new)
        wk = jnp.exp(g_col - m_new) * kh.astype(F32)
        c_ref[h] = decay * c_old + _dot(wk.T.astype(BF16), v_aug)
        m_ref[h] = jnp.broadcast_to(m_new, m_ref.shape[1:])

        hn = hc * lax.rsqrt(jnp.mean(hc * hc, axis=-1, keepdims=True) + EPS) * g_ref[:, lanes]
        o_ref[0, :, lanes] = (hn * jax.nn.sigmoid(mo_ref[0, :, lanes])).astype(BF16)


def _mlstm(mq, mk, mv, mo, sm, g):
    B, TP, _ = mq.shape
    tok = lambda b, c: (b, c, 0)
    return pl.pallas_call(
        _mlstm_kernel,
        grid=(B, TP // LANE),
        in_specs=[
            pl.BlockSpec((1, LANE, BRANCH_W), tok),
            pl.BlockSpec((1, LANE, BRANCH_W), tok),
            pl.BlockSpec((1, LANE, BRANCH_W), tok),
            pl.BlockSpec((1, LANE, BRANCH_W), tok),
            pl.BlockSpec((1, LANE, SEG_SM), tok),
            pl.BlockSpec((1, BRANCH_W), lambda b, c: (0, 0)),
        ],
        out_specs=pl.BlockSpec((1, LANE, BRANCH_W), tok),
        out_shape=jax.ShapeDtypeStruct((B, TP, BRANCH_W), BF16),
        scratch_shapes=[
            pltpu.VMEM((ML_HEADS, ML_DIM, 2 * ML_DIM), F32),
            pltpu.VMEM((ML_HEADS, 8, LANE), F32),
        ],
        compiler_params=_params("arbitrary", "arbitrary"),
        name="mlstm",
    )(mq, mk, mv, mo, sm, g)


def _merge_kernel(h_ref, ya_ref, yb_ref, yc_ref, gin_ref, gout_ref, wgt_ref, wbr_ref, wo_ref, o_ref):
    x = h_ref[...]
    xn = _rms(x, gin_ref[...]).astype(BF16)
    merged = None
    for n, y_ref in enumerate((ya_ref, yb_ref, yc_ref)):
        gate = jax.nn.sigmoid(_dot(xn, wgt_ref[:, n * D_MODEL:(n + 1) * D_MODEL]))
        term = gate * _dot(y_ref[...], wbr_ref[n])
        merged = term if merged is None else merged + term
    y = _dot(merged.astype(BF16), wo_ref[...])
    o_ref[...] = x + _rms(y, gout_ref[...])


def _merge(h2, ya, yb, yc, g_in, g_out, w_gt, w_br, w_o):
    rows = h2.shape[0]
    tm = _pick_tile(rows, (512, 384, 256, 128))
    const2 = lambda i: (0, 0)
    tok = lambda i: (i, 0)
    return pl.pallas_call(
        _merge_kernel,
        grid=(rows // tm,),
        in_specs=[
            pl.BlockSpec((tm, D_MODEL), tok),
            pl.BlockSpec((tm, BRANCH_W), tok),
            pl.BlockSpec((tm, BRANCH_W), tok),
            pl.BlockSpec((tm, BRANCH_W), tok),
            pl.BlockSpec((1, D_MODEL), const2),
            pl.BlockSpec((1, D_MODEL), const2),
            pl.BlockSpec((D_MODEL, N_BRANCH * D_MODEL), const2, pipeline_mode=pl.Buffered(1)),
            pl.BlockSpec((N_BRANCH, BRANCH_W, D_MODEL), lambda i: (0, 0, 0), pipeline_mode=pl.Buffered(1)),
            pl.BlockSpec((D_MODEL, D_MODEL), const2, pipeline_mode=pl.Buffered(1)),
        ],
        out_specs=pl.BlockSpec((tm, D_MODEL), tok),
        out_shape=jax.ShapeDtypeStruct((rows, D_MODEL), F32),
        compiler_params=_params("arbitrary"),
        name="merge",
    )(h2, ya, yb, yc, g_in, g_out, w_gt, w_br, w_o)


def _rot_cols(w):
    d, n = w.shape
    w = w.reshape(d, n // HEAD_DIM, 2, HEAD_DIM // 2)
    return jnp.stack([-w[:, :, 1], w[:, :, 0]], axis=2).reshape(d, n)


def _pad_heads(w):
    d, n = w.shape
    w = w.reshape(d, n // HEAD_DIM, HEAD_DIM)
    return jnp.pad(w, ((0, 0), (0, 0), (0, LANE - HEAD_DIM))).reshape(d, -1)


def _pack_w_in(w_in):
    sizes = (BRANCH_W, BRANCH_W, BRANCH_W, ATT_HEADS * HEAD_DIM, ATT_KV_HEADS * HEAD_DIM,
             ATT_KV_HEADS * HEAD_DIM, IDX_HEADS * HEAD_DIM, HEAD_DIM, IDX_HEADS,
             BRANCH_W, BRANCH_W, BRANCH_W, BRANCH_W, ML_HEADS, ML_HEADS, N_BRANCH * D_MODEL)
    offs = np.cumsum(sizes)[:-1].tolist()
    (cx, cb, cc, q, k, v, qi, ki, wi, mq, mk, mv, mo, mi, mf, gt) = jnp.split(w_in, offs, axis=1)
    d = w_in.shape[0]
    zeros64 = jnp.zeros((d, HEAD_DIM), w_in.dtype)
    v_parts = []
    for g in range(ATT_KV_HEADS):
        vg = v[:, g * HEAD_DIM:(g + 1) * HEAD_DIM]
        v_parts += [vg, zeros64, zeros64, vg]
    small = jnp.concatenate([wi, mi, mf, jnp.zeros((d, SEG_SM - 3 * IDX_HEADS), w_in.dtype)], axis=1)
    packed = jnp.concatenate([
        cx, cb, cc,
        _pad_heads(q), _pad_heads(_rot_cols(q)),
        _pad_heads(k), _pad_heads(_rot_cols(k)),
        *v_parts,
        _pad_heads(qi), _pad_heads(_rot_cols(qi)),
        _pad_heads(ki), _pad_heads(_rot_cols(ki)),
        small, mq, mk, mv, mo], axis=1)
    assert packed.shape[1] == W_PACKED
    return packed.astype(BF16), gt.astype(BF16)


def _rope_tables(tp):
    inv = 1.0 / (ROPE_THETA ** (jnp.arange(0, HEAD_DIM, 2, dtype=F32) / HEAD_DIM))
    ang = jnp.arange(tp, dtype=F32)[:, None] * inv[None, :]
    reps = 2 * LANE // HEAD_DIM
    return jnp.tile(jnp.cos(ang), (1, reps)), jnp.tile(jnp.sin(ang), (1, reps))


def kernel(x, meta, norm_g, w_ffn_gu, w_ffn_down, w_in, w_conv, b_igate, b_fgate, mh_norm_g, w_branch, w_out):
    B, S, _ = x.shape
    depth = norm_g.shape[0]
    T = N_META + S
    TP = -(-T // LANE) * LANE
    topk = min(IDX_TOPK_MAX, S // 4)

    h = jnp.concatenate([jnp.broadcast_to(meta.astype(x.dtype)[None], (B, N_META, D_MODEL)), x,
                         jnp.zeros((B, TP - T, D_MODEL), x.dtype)], axis=1)
    cos, sin = _rope_tables(TP)
    vone = np.zeros((1, SEG_V), np.float32)
    for g in range(ATT_KV_HEADS):
        vone[0, (2 * g) * LANE + HEAD_DIM] = 1.0
        vone[0, (2 * g + 1) * LANE] = 1.0
    vone = jnp.asarray(vone)

    h2 = h.reshape(B * TP, D_MODEL)
    for l in range(depth):
        g = norm_g[l][:, None, :]
        w_packed, w_gt = _pack_w_in(w_in[l])
        smb = jnp.concatenate([jnp.zeros((IDX_HEADS,), F32), b_igate[l], b_fgate[l],
                               jnp.zeros((SEG_SM - 3 * IDX_HEADS,), F32)])[None, :]
        wconv = jnp.pad(w_conv[l], ((0, 8 - CONV_K), (0, 0)))

        h2 = _ffn(h2, g[0], g[1], w_ffn_gu[l, 0].astype(BF16), w_ffn_down[l, 0].astype(BF16))
        (ya, q, k, v, qi, ki, sm, mq, mk, mv, mo) = _inproj(
            h2.reshape(B, TP, D_MODEL), g[2], cos, sin, wconv, smb, vone, w_packed)
        yb = _attention(q, qi, sm, k, v, ki, topk)
        yc = _mlstm(mq, mk, mv, mo, sm, mh_norm_g[l][None, :])
        flat = lambda a: a.reshape(B * TP, BRANCH_W)
        h2 = _merge(h2, flat(ya), flat(yb), flat(yc), g[2], g[3], w_gt,
                    w_branch[l].astype(BF16), w_out[l].astype(BF16))
        h2 = _ffn(h2, g[4], g[5], w_ffn_gu[l, 1].astype(BF16), w_ffn_down[l, 1].astype(BF16))
    return h2.reshape(B, TP, D_MODEL)[:, N_META:T]
```

```python
import functools

import numpy as np
import jax
import jax.numpy as jnp
from jax import lax
from jax.experimental import pallas as pl
from jax.experimental.pallas import tpu as pltpu

F32 = jnp.float32
BF16 = jnp.bfloat16

D_MODEL = 1024
N_META = 16
EPS = 1e-6
D_FF = 11 * D_MODEL // 4
BRANCH_W = D_MODEL // 2
N_BRANCH = 3
CONV_K = 3
HEAD_DIM = 64
ATT_HEADS = 8
ATT_KV_HEADS = 2
ATT_REP = ATT_HEADS // ATT_KV_HEADS
IDX_HEADS = 4
IDX_TOPK_MAX = 256
ML_HEADS = 4
ML_DIM = 128
ROPE_THETA = 10000.0

LANE = 128
FF_CHUNK = 256
SEQ_ALIGN = LANE
VMEM_LIMIT = 56 * 1024 * 1024
Q_SCALE = HEAD_DIM ** -0.5 * 1.4426950408889634
INT_MIN = -2 ** 31
NEG_BIG = -1e30

SEG_CONV = 3 * BRANCH_W
SEG_Q = ATT_HEADS * LANE
SEG_K = ATT_KV_HEADS * LANE
SEG_V = 2 * ATT_KV_HEADS * LANE
SEG_QI = IDX_HEADS * LANE
SEG_KI = LANE
SEG_SM = LANE
SEG_ML = 4 * BRANCH_W
OFF_CONV = 0
OFF_Q = OFF_CONV + SEG_CONV
OFF_QR = OFF_Q + SEG_Q
OFF_K = OFF_QR + SEG_Q
OFF_KR = OFF_K + SEG_K
OFF_V = OFF_KR + SEG_K
OFF_QI = OFF_V + SEG_V
OFF_QIR = OFF_QI + SEG_QI
OFF_KI = OFF_QIR + SEG_QI
OFF_KIR = OFF_KI + SEG_KI
OFF_SM = OFF_KIR + SEG_KI
OFF_ML = OFF_SM + SEG_SM
W_PACKED = OFF_ML + SEG_ML


def _pick_tile(n, candidates):
    for c in candidates:
        if n % c == 0:
            return c
    raise ValueError(f"no tile for {n}")


def _rms(x, g):
    return x * lax.rsqrt(jnp.mean(x * x, axis=-1, keepdims=True) + EPS) * g


def _dot(a, b):
    return jnp.dot(a, b, preferred_element_type=F32)


def _dot_nt(a, b):
    return lax.dot_general(a, b, (((1,), (1,)), ((), ())), preferred_element_type=F32)


def _params(*sem):
    return pltpu.CompilerParams(dimension_semantics=sem, vmem_limit_bytes=VMEM_LIMIT)


def _ffn_kernel(h_ref, gin_ref, gout_ref, wgu_ref, wd_ref, o_ref, act_ref):
    x = h_ref[...]
    xn = _rms(x, gin_ref[...]).astype(BF16)
    for c in range(D_FF // FF_CHUNK):
        lo = c * FF_CHUNK
        gate = _dot(xn, wgu_ref[:, lo:lo + FF_CHUNK])
        up = _dot(xn, wgu_ref[:, D_FF + lo:D_FF + lo + FF_CHUNK])
        act_ref[:, lo:lo + FF_CHUNK] = (gate * jax.nn.sigmoid(gate) * up).astype(BF16)
    y = _dot(act_ref[...], wd_ref[...])
    o_ref[...] = x + 0.5 * _rms(y, gout_ref[...])


def _ffn(h2, g_in, g_out, w_gu, w_down):
    rows = h2.shape[0]
    tm = _pick_tile(rows, (512, 384, 256, 128))
    const = lambda i: (0, 0)
    return pl.pallas_call(
        _ffn_kernel,
        grid=(rows // tm,),
        in_specs=[
            pl.BlockSpec((tm, D_MODEL), lambda i: (i, 0)),
            pl.BlockSpec((1, D_MODEL), const),
            pl.BlockSpec((1, D_MODEL), const),
            pl.BlockSpec((D_MODEL, 2 * D_FF), const, pipeline_mode=pl.Buffered(1)),
            pl.BlockSpec((D_FF, D_MODEL), const, pipeline_mode=pl.Buffered(1)),
        ],
        out_specs=pl.BlockSpec((tm, D_MODEL), lambda i: (i, 0)),
        out_shape=jax.ShapeDtypeStruct((rows, D_MODEL), F32),
        scratch_shapes=[pltpu.VMEM((tm, D_FF), BF16)],
        compiler_params=_params("arbitrary"),
        name="ffn",
    )(h2, g_in, g_out, w_gu, w_down)


def _inproj_kernel(h_ref, g_ref, cos_ref, sin_ref, wconv_ref, smb_ref, vone_ref, w_ref,
                   ya_ref, q_ref, k_ref, v_ref, qi_ref, ki_ref, sm_ref,
                   mq_ref, mk_ref, mv_ref, mo_ref, carry_ref):
    t = pl.program_id(1)
    tm = h_ref.shape[1]

    @pl.when(t == 0)
    def _():
        carry_ref[...] = jnp.zeros_like(carry_ref)

    xn = _rms(h_ref[0], g_ref[...]).astype(BF16)

    def proj(off, width):
        return _dot(xn, w_ref[:, off:off + width])

    cx = proj(OFF_CONV, BRANCH_W)
    cc = proj(OFF_CONV + 2 * BRANCH_W, BRANCH_W)
    z = cc * cx
    row = lax.broadcasted_iota(jnp.int32, z.shape, 0)
    prev1 = jnp.broadcast_to(carry_ref[7:8, :], z.shape)
    prev2 = jnp.broadcast_to(carry_ref[6:7, :], z.shape)
    z1 = jnp.where(row == 0, prev1, pltpu.roll(z, 1, 0))
    z2 = jnp.where(row == 0, prev2, jnp.where(row == 1, prev1, pltpu.roll(z, 2, 0)))
    carry_ref[...] = z[tm - 8:tm, :]
    conv = wconv_ref[0:1, :] * z2 + wconv_ref[1:2, :] * z1 + wconv_ref[2:3, :] * z
    cb = proj(OFF_CONV + BRANCH_W, BRANCH_W)
    ya_ref[0] = (cb * conv).astype(BF16)

    cos = cos_ref[...]
    sin = sin_ref[...]

    def rope_to(out_ref, off, off_rot, width, scale, transposed):
        for c in range(width // LANE):
            x = proj(off + c * LANE, LANE)
            xr = proj(off_rot + c * LANE, LANE)
            y = x * cos + xr * sin
            if scale != 1.0:
                y = y * scale
            if transposed:
                out_ref[0, c * LANE:(c + 1) * LANE, :] = y.T.astype(BF16)
            else:
                out_ref[0, :, c * LANE:(c + 1) * LANE] = y.astype(BF16)

    rope_to(q_ref, OFF_Q, OFF_QR, SEG_Q, Q_SCALE, False)
    rope_to(k_ref, OFF_K, OFF_KR, SEG_K, 1.0, True)
    rope_to(qi_ref, OFF_QI, OFF_QIR, SEG_QI, 1.0, False)
    rope_to(ki_ref, OFF_KI, OFF_KIR, SEG_KI, 1.0, True)
    v_ref[0] = (proj(OFF_V, SEG_V) + vone_ref[...]).astype(BF16)

    raw = proj(OFF_SM, SEG_SM) + smb_ref[...]
    col = lax.broadcasted_iota(jnp.int32, raw.shape, 1)
    logsig = jnp.minimum(raw, 0.0) - jnp.log(1.0 + jnp.exp(-jnp.abs(raw)))
    idx_scale = (IDX_HEADS ** -0.5) * (HEAD_DIM ** -0.5)
    sm_ref[0] = jnp.where(col < IDX_HEADS, raw * idx_scale,
                          jnp.where(col < 2 * IDX_HEADS, raw,
                                    jnp.where(col < 3 * IDX_HEADS, logsig, 0.0)))

    mq_ref[0] = proj(OFF_ML, BRANCH_W).astype(BF16)
    mk_ref[0] = (proj(OFF_ML + BRANCH_W, BRANCH_W) * (ML_DIM ** -0.5)).astype(BF16)
    mv_ref[0] = proj(OFF_ML + 2 * BRANCH_W, BRANCH_W).astype(BF16)
    mo_ref[0] = proj(OFF_ML + 3 * BRANCH_W, BRANCH_W)


def _inproj(h3, g, cos, sin, wconv, smb, vone, w_packed):
    B, TP, _ = h3.shape
    tm = _pick_tile(TP, (384, 256, 128))
    const = lambda b, t: (0, 0)
    tok = lambda b, t: (b, t, 0)

    def out(width, dtype):
        return (pl.BlockSpec((1, tm, width), tok), jax.ShapeDtypeStruct((B, TP, width), dtype))

    def out_t(width, dtype):
        return (pl.BlockSpec((1, width, tm), lambda b, t: (b, 0, t)),
                jax.ShapeDtypeStruct((B, width, TP), dtype))

    outs = [out(BRANCH_W, BF16), out(SEG_Q, BF16), out_t(SEG_K, BF16), out(SEG_V, BF16),
            out(SEG_QI, BF16), out_t(SEG_KI, BF16), out(SEG_SM, F32),
            out(BRANCH_W, BF16), out(BRANCH_W, BF16), out(BRANCH_W, BF16), out(BRANCH_W, F32)]
    return pl.pallas_call(
        _inproj_kernel,
        grid=(B, TP // tm),
        in_specs=[
            pl.BlockSpec((1, tm, D_MODEL), tok),
            pl.BlockSpec((1, D_MODEL), const),
            pl.BlockSpec((tm, LANE), lambda b, t: (t, 0)),
            pl.BlockSpec((tm, LANE), lambda b, t: (t, 0)),
            pl.BlockSpec((8, BRANCH_W), const),
            pl.BlockSpec((1, SEG_SM), const),
            pl.BlockSpec((1, SEG_V), const),
            pl.BlockSpec((D_MODEL, W_PACKED), const, pipeline_mode=pl.Buffered(1)),
        ],
        out_specs=[o[0] for o in outs],
        out_shape=[o[1] for o in outs],
        scratch_shapes=[pltpu.VMEM((8, BRANCH_W), F32)],
        compiler_params=_params("arbitrary", "arbitrary"),
        name="inproj",
    )(h3, g, cos, sin, wconv, smb, vone, w_packed)


def _score_key(sc):
    bits = lax.bitcast_convert_type(sc, jnp.int32)
    return jnp.where(bits < 0, INT_MIN - bits, bits)


def _attn_kernel(q_ref, qi_ref, sm_ref, k_ref, v_ref, ki_ref, o_ref,
                 keys_ref, wb_ref, acc_ref, m_ref, thr_ref, q4_ref, qi4_ref, *, topk, ub):
    i = pl.program_id(1)
    blk = LANE
    uw = ub * blk
    nunits = i + 1
    row = lax.broadcasted_iota(jnp.int32, (uw, blk), 0)
    col = lax.broadcasted_iota(jnp.int32, (uw, blk), 1)

    def unit(u):
        return pl.ds(pl.multiple_of(u * uw, blk), uw)

    def block(u, cb):
        return pl.ds(pl.multiple_of(u * uw + cb * blk, blk), blk)

    for h in range(IDX_HEADS):
        wb_ref[h] = jnp.broadcast_to(sm_ref[0, :, h:h + 1], (uw, blk))
        qi4_ref[h * uw:(h + 1) * uw, :] = qi_ref[0, :, h * LANE:(h + 1) * LANE]
    qpos = i * uw + row

    def p1(u, c):
        rel = _dot(qi4_ref[...], ki_ref[0, :, unit(u)])
        for cb in range(ub):
            lanes = slice(cb * blk, (cb + 1) * blk)
            sc = wb_ref[0] * jnp.maximum(rel[0:uw, lanes], 0.0)
            for h in range(1, IDX_HEADS):
                sc = sc + wb_ref[h] * jnp.maximum(rel[h * uw:(h + 1) * uw, lanes], 0.0)
            kpos = u * uw + cb * blk + col
            keys_ref[:, block(u, cb)] = jnp.where(kpos <= qpos, _score_key(sc), INT_MIN)
        return c

    lax.fori_loop(0, nunits, p1, 0)

    rows_of = [slice(rb * blk, (rb + 1) * blk) for rb in range(ub)]
    col1 = lax.broadcasted_iota(jnp.int32, (blk, blk), 1)

    def lane_counts(rb, pred):
        def body(u, acc):
            for cb in range(ub):
                hit = pred(keys_ref[rows_of[rb], block(u, cb)], u * uw + cb * blk)
                acc = acc + jnp.where(hit, 1.0, 0.0)
            return acc
        return lax.fori_loop(0, nunits, body, jnp.zeros((blk, blk), F32))

    def count_where(rb, pred):
        return jnp.sum(lane_counts(rb, pred), axis=-1, keepdims=True)

    def ge(cand):
        cand_b = jnp.broadcast_to(cand, (blk, blk))
        return lambda kb, base: kb >= cand_b

    def count_ge(rb, cand):
        return count_where(rb, ge(cand))

    kf = float(topk)
    zero = jnp.zeros((blk, 1), jnp.int32)
    r0 = tuple(jnp.where(count_ge(rb, zero) >= kf, zero, INT_MIN) for rb in range(ub))

    def bit_step(t, rs):
        bit = jnp.left_shift(jnp.int32(1), 30 - t)
        cands = [rs[rb] | bit for rb in range(ub)]
        accs = [lane_counts(rb, ge(cands[rb])) for rb in range(ub)]
        return tuple(jnp.where(jnp.sum(accs[rb], axis=-1, keepdims=True) >= kf, cands[rb], rs[rb])
                     for rb in range(ub))

    rs = lax.fori_loop(0, 31, bit_step, r0)
    ties = [jnp.logical_and(rs[rb] > INT_MIN, count_ge(rb, rs[rb]) > kf) for rb in range(ub)]
    any_tie = jnp.max(jnp.where(ties[0], 1.0, 0.0))
    for rb in range(1, ub):
        any_tie = jnp.maximum(any_tie, jnp.max(jnp.where(ties[rb], 1.0, 0.0)))

    @pl.when(any_tie > 0.0)
    def _():
        for rb in range(ub):
            r_b = jnp.broadcast_to(rs[rb], (blk, blk))
            tie_b = jnp.broadcast_to(ties[rb], (blk, blk))
            need = kf - count_ge(rb, rs[rb] + 1)

            def pos_step(t, pos):
                cand = pos + jnp.left_shift(jnp.int32(1), 12 - t)
                cand_b = jnp.broadcast_to(cand, (blk, blk))
                cnt = count_where(rb, lambda kb, base: jnp.logical_and(kb == r_b, col1 + base < cand_b))
                return jnp.where(cnt < need, cand, pos)

            pos_b = jnp.broadcast_to(lax.fori_loop(0, 13, pos_step, zero), (blk, blk))

            def fix(u, c):
                for cb in range(ub):
                    kb = keys_ref[rows_of[rb], block(u, cb)]
                    late = col1 + (u * uw + cb * blk) > pos_b
                    drop = jnp.logical_and(jnp.logical_and(kb == r_b, late), tie_b)
                    keys_ref[rows_of[rb], block(u, cb)] = jnp.where(drop, r_b - 1, kb)
                return c

            lax.fori_loop(0, nunits, fix, 0)

    for rb in range(ub):
        thr_ref[rows_of[rb], :] = jnp.broadcast_to(jnp.maximum(rs[rb], INT_MIN + 1), (blk, blk))
    thr_b = thr_ref[...]

    m_ref[...] = jnp.full(m_ref.shape, NEG_BIG, F32)
    acc_ref[...] = jnp.zeros(acc_ref.shape, F32)
    for g in range(ATT_KV_HEADS):
        for rr in range(ATT_REP):
            h = g * ATT_REP + rr
            q4_ref[g, rr * uw:(rr + 1) * uw, :] = q_ref[0, :, h * LANE:(h + 1) * LANE]

    def p3(u, c):
        bias = [jnp.where(keys_ref[:, block(u, cb)] >= thr_b, 0.0, NEG_BIG) for cb in range(ub)]
        for g in range(ATT_KV_HEADS):
            s4 = _dot(q4_ref[g], k_ref[0, g * LANE:(g + 1) * LANE, unit(u)])
            ps, alphas = [], []
            for rr in range(ATT_REP):
                h = g * ATT_REP + rr
                s = [s4[rr * uw:(rr + 1) * uw, cb * blk:(cb + 1) * blk] + bias[cb] for cb in range(ub)]
                smax = s[0]
                for cb in range(1, ub):
                    smax = jnp.maximum(smax, s[cb])
                m_old = m_ref[h]
                m_new = jnp.maximum(m_old, jnp.max(smax, axis=-1, keepdims=True))
                ps.append(jnp.concatenate([jnp.exp2(sc - m_new).astype(BF16) for sc in s], axis=1))
                alphas.append(jnp.exp2(m_old - m_new))
                m_ref[h] = m_new
            for par in range(2):
                pv = _dot(jnp.concatenate([ps[par], ps[par + 2]], axis=0),
                          v_ref[0, unit(u), (2 * g + par) * LANE:(2 * g + par + 1) * LANE])
                for n, rr in enumerate((par, par + 2)):
                    h = g * ATT_REP + rr
                    acc_ref[h] = alphas[rr] * acc_ref[h] + pv[n * uw:(n + 1) * uw]
        return c

    lax.fori_loop(0, nunits, p3, 0)

    for pr in range(ATT_HEADS // 2):
        ae = acc_ref[2 * pr]
        ao = acc_ref[2 * pr + 1]
        oe = ae / ae[:, HEAD_DIM:HEAD_DIM + 1]
        oo = ao / ao[:, 0:1]
        o_ref[0, :, pr * LANE:(pr + 1) * LANE] = jnp.where(col < HEAD_DIM, oe, oo).astype(BF16)


def _attention(q, qi, sm, k, v, ki, topk):
    B, TP, _ = q.shape
    nblk = TP // LANE
    ub = _pick_tile(nblk, (3, 2, 1))
    uw = ub * LANE
    assert TP <= 2 ** 13
    qmap = lambda b, i: (b, i, 0)
    kmap = lambda b, i: (b, 0, 0)
    return pl.pallas_call(
        functools.partial(_attn_kernel, topk=topk, ub=ub),
        grid=(B, TP // uw),
        in_specs=[
            pl.BlockSpec((1, uw, SEG_Q), qmap),
            pl.BlockSpec((1, uw, SEG_QI), qmap),
            pl.BlockSpec((1, uw, SEG_SM), qmap),
            pl.BlockSpec((1, SEG_K, TP), kmap),
            pl.BlockSpec((1, TP, SEG_V), kmap),
            pl.BlockSpec((1, SEG_KI, TP), kmap),
        ],
        out_specs=pl.BlockSpec((1, uw, BRANCH_W), qmap),
        out_shape=jax.ShapeDtypeStruct((B, TP, BRANCH_W), BF16),
        scratch_shapes=[
            pltpu.VMEM((uw, TP), jnp.int32),
            pltpu.VMEM((IDX_HEADS, uw, LANE), F32),
            pltpu.VMEM((ATT_HEADS, uw, LANE), F32),
            pltpu.VMEM((ATT_HEADS, uw, LANE), F32),
            pltpu.VMEM((uw, LANE), jnp.int32),
            pltpu.VMEM((ATT_KV_HEADS, ATT_REP * uw, LANE), BF16),
            pltpu.VMEM((IDX_HEADS * uw, LANE), BF16),
        ],
        compiler_params=_params("arbitrary", "arbitrary"),
        name="attn",
    )(q, qi, sm, k, v, ki)


def _mlstm_kernel(q_ref, k_ref, v_ref, mo_ref, sm_ref, g_ref, o_ref, c_ref, m_ref):
    L = LANE

    @pl.when(pl.program_id(1) == 0)
    def _():
        c_ref[...] = jnp.zeros_like(c_ref)
        m_ref[...] = jnp.zeros_like(m_ref)

    row = lax.broadcasted_iota(jnp.int32, (L, L), 0)
    col = lax.broadcasted_iota(jnp.int32, (L, L), 1)
    tril = col <= row
    sm = sm_ref[0]
    cum = jnp.dot(jnp.where(tril, 1.0, 0.0), sm, precision=lax.Precision.HIGHEST,
                  preferred_element_type=F32)
    sm_t = sm.T
    cum_t = cum.T
    ones_col = jnp.where(col == 0, 1.0, 0.0).astype(BF16)

    for h in range(ML_HEADS):
        lanes = slice(h * ML_DIM, (h + 1) * ML_DIM)
        ci, cf = IDX_HEADS + h, 2 * IDX_HEADS + h
        b_col = cum[:, cf:cf + 1]
        b_row = cum_t[cf:cf + 1, :]
        b_last = cum[L - 1:L, cf:cf + 1]
        i_col = sm[:, ci:ci + 1]
        i_row = sm_t[ci:ci + 1, :]
        m_old = m_ref[h][0:1, 0:1]
        qh = q_ref[0, :, lanes]
        kh = k_ref[0, :, lanes]
        v_aug = jnp.concatenate([v_ref[0, :, lanes], ones_col], axis=1)

        dm = jnp.where(tril, b_col - b_row + i_row, -jnp.inf)
        a = b_col + m_old
        mq = jnp.maximum(a, jnp.max(dm, axis=-1, keepdims=True))
        inter = jnp.exp(a - mq)
        wqk = jnp.exp(dm - mq) * _dot_nt(qh, kh)
        c_old = c_ref[h]
        res = inter * _dot(qh, c_old.astype(BF16)) + _dot(wqk.astype(BF16), v_aug)
        den = res[:, ML_DIM:ML_DIM + 1]
        hc = res[:, 0:ML_DIM] / jnp.maximum(jnp.abs(den), jnp.exp(-mq))

        g_col = b_last - b_col + i_col
        m_new = jnp.maximum(b_last + m_old, jnp.max(g_col, axis=0, keepdims=True))
        decay = jnp.exp(b_last + m_old - m_new)
        wk = jnp.exp(g_col - m_new) * kh.astype(F32)
        c_ref[h] = decay * c_old + _dot(wk.T.astype(BF16), v_aug)
        m_ref[h] = jnp.broadcast_to(m_new, m_ref.shape[1:])

        hn = hc * lax.rsqrt(jnp.mean(hc * hc, axis=-1, keepdims=True) + EPS) * g_ref[:, lanes]
        o_ref[0, :, lanes] = (hn * jax.nn.sigmoid(mo_ref[0, :, lanes])).astype(BF16)


def _mlstm(mq, mk, mv, mo, sm, g):
    B, TP, _ = mq.shape
    tok = lambda b, c: (b, c, 0)
    return pl.pallas_call(
        _mlstm_kernel,
        grid=(B, TP // LANE),
        in_specs=[
            pl.BlockSpec((1, LANE, BRANCH_W), tok),
            pl.BlockSpec((1, LANE, BRANCH_W), tok),
            pl.BlockSpec((1, LANE, BRANCH_W), tok),
            pl.BlockSpec((1, LANE, BRANCH_W), tok),
            pl.BlockSpec((1, LANE, SEG_SM), tok),
            pl.BlockSpec((1, BRANCH_W), lambda b, c: (0, 0)),
        ],
        out_specs=pl.BlockSpec((1, LANE, BRANCH_W), tok),
        out_shape=jax.ShapeDtypeStruct((B, TP, BRANCH_W), BF16),
        scratch_shapes=[
            pltpu.VMEM((ML_HEADS, ML_DIM, 2 * ML_DIM), F32),
            pltpu.VMEM((ML_HEADS, 8, LANE), F32),
        ],
        compiler_params=_params("arbitrary", "arbitrary"),
        name="mlstm",
    )(mq, mk, mv, mo, sm, g)


def _merge_kernel(h_ref, ya_ref, yb_ref, yc_ref, gin_ref, gout_ref, wgt_ref, wbr_ref, wo_ref, o_ref):
    x = h_ref[...]
    xn = _rms(x, gin_ref[...]).astype(BF16)
    merged = None
    for n, y_ref in enumerate((ya_ref, yb_ref, yc_ref)):
        gate = jax.nn.sigmoid(_dot(xn, wgt_ref[:, n * D_MODEL:(n + 1) * D_MODEL]))
        term = gate * _dot(y_ref[...], wbr_ref[n])
        merged = term if merged is None else merged + term
    y = _dot(merged.astype(BF16), wo_ref[...])
    o_ref[...] = x + _rms(y, gout_ref[...])


def _merge(h2, ya, yb, yc, g_in, g_out, w_gt, w_br, w_o):
    rows = h2.shape[0]
    tm = _pick_tile(rows, (512, 384, 256, 128))
    const2 = lambda i: (0, 0)
    tok = lambda i: (i, 0)
    return pl.pallas_call(
        _merge_kernel,
        grid=(rows // tm,),
        in_specs=[
            pl.BlockSpec((tm, D_MODEL), tok),
            pl.BlockSpec((tm, BRANCH_W), tok),
            pl.BlockSpec((tm, BRANCH_W), tok),
            pl.BlockSpec((tm, BRANCH_W), tok),
            pl.BlockSpec((1, D_MODEL), const2),
            pl.BlockSpec((1, D_MODEL), const2),
            pl.BlockSpec((D_MODEL, N_BRANCH * D_MODEL), const2, pipeline_mode=pl.Buffered(1)),
            pl.BlockSpec((N_BRANCH, BRANCH_W, D_MODEL), lambda i: (0, 0, 0), pipeline_mode=pl.Buffered(1)),
            pl.BlockSpec((D_MODEL, D_MODEL), const2, pipeline_mode=pl.Buffered(1)),
        ],
        out_specs=pl.BlockSpec((tm, D_MODEL), tok),
        out_shape=jax.ShapeDtypeStruct((rows, D_MODEL), F32),
        compiler_params=_params("arbitrary"),
        name="merge",
    )(h2, ya, yb, yc, g_in, g_out, w_gt, w_br, w_o)


def _rot_cols(w):
    d, n = w.shape
    w = w.reshape(d, n // HEAD_DIM, 2, HEAD_DIM // 2)
    return jnp.stack([-w[:, :, 1], w[:, :, 0]], axis=2).reshape(d, n)


def _pad_heads(w):
    d, n = w.shape
    w = w.reshape(d, n // HEAD_DIM, HEAD_DIM)
    return jnp.pad(w, ((0, 0), (0, 0), (0, LANE - HEAD_DIM))).reshape(d, -1)


def _pack_w_in(w_in):
    sizes = (BRANCH_W, BRANCH_W, BRANCH_W, ATT_HEADS * HEAD_DIM, ATT_KV_HEADS * HEAD_DIM,
             ATT_KV_HEADS * HEAD_DIM, IDX_HEADS * HEAD_DIM, HEAD_DIM, IDX_HEADS,
             BRANCH_W, BRANCH_W, BRANCH_W, BRANCH_W, ML_HEADS, ML_HEADS, N_BRANCH * D_MODEL)
    offs = np.cumsum(sizes)[:-1].tolist()
    (cx, cb, cc, q, k, v, qi, ki, wi, mq, mk, mv, mo, mi, mf, gt) = jnp.split(w_in, offs, axis=1)
    d = w_in.shape[0]
    zeros64 = jnp.zeros((d, HEAD_DIM), w_in.dtype)
    v_parts = []
    for g in range(ATT_KV_HEADS):
        vg = v[:, g * HEAD_DIM:(g + 1) * HEAD_DIM]
        v_parts += [vg, zeros64, zeros64, vg]
    small = jnp.concatenate([wi, mi, mf, jnp.zeros((d, SEG_SM - 3 * IDX_HEADS), w_in.dtype)], axis=1)
    packed = jnp.concatenate([
        cx, cb, cc,
        _pad_heads(q), _pad_heads(_rot_cols(q)),
        _pad_heads(k), _pad_heads(_rot_cols(k)),
        *v_parts,
        _pad_heads(qi), _pad_heads(_rot_cols(qi)),
        _pad_heads(ki), _pad_heads(_rot_cols(ki)),
        small, mq, mk, mv, mo], axis=1)
    assert packed.shape[1] == W_PACKED
    return packed.astype(BF16), gt.astype(BF16)


def _rope_tables(tp):
    inv = 1.0 / (ROPE_THETA ** (jnp.arange(0, HEAD_DIM, 2, dtype=F32) / HEAD_DIM))
    ang = jnp.arange(tp, dtype=F32)[:, None] * inv[None, :]
    reps = 2 * LANE // HEAD_DIM
    return jnp.tile(jnp.cos(ang), (1, reps)), jnp.tile(jnp.sin(ang), (1, reps))


def kernel(x, meta, norm_g, w_ffn_gu, w_ffn_down, w_in, w_conv, b_igate, b_fgate, mh_norm_g, w_branch, w_out):
    B, S, _ = x.shape
    depth = norm_g.shape[0]
    T = N_META + S
    TP = -(-T // SEQ_ALIGN) * SEQ_ALIGN
    topk = min(IDX_TOPK_MAX, S // 4)

    h = jnp.concatenate([jnp.broadcast_to(meta.astype(x.dtype)[None], (B, N_META, D_MODEL)), x,
                         jnp.zeros((B, TP - T, D_MODEL), x.dtype)], axis=1)
    cos, sin = _rope_tables(TP)
    vone = np.zeros((1, SEG_V), np.float32)
    for g in range(ATT_KV_HEADS):
        vone[0, (2 * g) * LANE + HEAD_DIM] = 1.0
        vone[0, (2 * g + 1) * LANE] = 1.0
    vone = jnp.asarray(vone)

    h2 = h.reshape(B * TP, D_MODEL)
    for l in range(depth):
        g = norm_g[l][:, None, :]
        w_packed, w_gt = _pack_w_in(w_in[l])
        smb = jnp.concatenate([jnp.zeros((IDX_HEADS,), F32), b_igate[l], b_fgate[l],
                               jnp.zeros((SEG_SM - 3 * IDX_HEADS,), F32)])[None, :]
        wconv = jnp.pad(w_conv[l], ((0, 8 - CONV_K), (0, 0)))

        h2 = _ffn(h2, g[0], g[1], w_ffn_gu[l, 0].astype(BF16), w_ffn_down[l, 0].astype(BF16))
        (ya, q, k, v, qi, ki, sm, mq, mk, mv, mo) = _inproj(
            h2.reshape(B, TP, D_MODEL), g[2], cos, sin, wconv, smb, vone, w_packed)
        yb = _attention(q, qi, sm, k, v, ki, topk)
        yc = _mlstm(mq, mk, mv, mo, sm, mh_norm_g[l][None, :])
        flat = lambda a: a.reshape(B * TP, BRANCH_W)
        h2 = _merge(h2, flat(ya), flat(yb), flat(yc), g[2], g[3], w_gt,
                    w_branch[l].astype(BF16), w_out[l].astype(BF16))
        h2 = _ffn(h2, g[4], g[5], w_ffn_gu[l, 1].astype(BF16), w_ffn_down[l, 1].astype(BF16))
    return h2.reshape(B, TP, D_MODEL)[:, N_META:T]
```

```python
import functools

import numpy as np
import jax
import jax.numpy as jnp
from jax import lax
from jax.experimental import pallas as pl
from jax.experimental.pallas import tpu as pltpu

F32 = jnp.float32
BF16 = jnp.bfloat16

D_MODEL = 1024
N_META = 16
EPS = 1e-6
D_FF = 11 * D_MODEL // 4
BRANCH_W = D_MODEL // 2
N_BRANCH = 3
CONV_K = 3
HEAD_DIM = 64
ATT_HEADS = 8
ATT_KV_HEADS = 2
ATT_REP = ATT_HEADS // ATT_KV_HEADS
IDX_HEADS = 4
IDX_TOPK_MAX = 256
ML_HEADS = 4
ML_DIM = 128
ROPE_THETA = 10000.0

LANE = 128
FF_CHUNK = 256
SEQ_ALIGN = 256
VMEM_LIMIT = 56 * 1024 * 1024
Q_SCALE = HEAD_DIM ** -0.5 * 1.4426950408889634
INT_MIN = -2 ** 31
NEG_BIG = -1e30

SEG_CONV = 3 * BRANCH_W
SEG_Q = ATT_HEADS * LANE
SEG_K = ATT_KV_HEADS * LANE
SEG_V = 2 * ATT_KV_HEADS * LANE
SEG_QI = IDX_HEADS * LANE
SEG_KI = LANE
SEG_SM = LANE
SEG_ML = 4 * BRANCH_W
OFF_CONV = 0
OFF_Q = OFF_CONV + SEG_CONV
OFF_QR = OFF_Q + SEG_Q
OFF_K = OFF_QR + SEG_Q
OFF_KR = OFF_K + SEG_K
OFF_V = OFF_KR + SEG_K
OFF_QI = OFF_V + SEG_V
OFF_QIR = OFF_QI + SEG_QI
OFF_KI = OFF_QIR + SEG_QI
OFF_KIR = OFF_KI + SEG_KI
OFF_SM = OFF_KIR + SEG_KI
OFF_ML = OFF_SM + SEG_SM
W_PACKED = OFF_ML + SEG_ML


def _pick_tile(n, candidates):
    for c in candidates:
        if n % c == 0:
            return c
    raise ValueError(f"no tile for {n}")


def _rms(x, g):
    return x * lax.rsqrt(jnp.mean(x * x, axis=-1, keepdims=True) + EPS) * g


def _dot(a, b):
    return jnp.dot(a, b, preferred_element_type=F32)


def _dot_nt(a, b):
    return lax.dot_general(a, b, (((1,), (1,)), ((), ())), preferred_element_type=F32)


def _params(*sem):
    return pltpu.CompilerParams(dimension_semantics=sem, vmem_limit_bytes=VMEM_LIMIT)


def _ffn_kernel(h_ref, gin_ref, gout_ref, wgu_ref, wd_ref, o_ref, act_ref):
    x = h_ref[...]
    xn = _rms(x, gin_ref[...]).astype(BF16)
    for c in range(D_FF // FF_CHUNK):
        lo = c * FF_CHUNK
        gate = _dot(xn, wgu_ref[:, lo:lo + FF_CHUNK])
        up = _dot(xn, wgu_ref[:, D_FF + lo:D_FF + lo + FF_CHUNK])
        act_ref[:, lo:lo + FF_CHUNK] = (gate * jax.nn.sigmoid(gate) * up).astype(BF16)
    y = _dot(act_ref[...], wd_ref[...])
    o_ref[...] = x + 0.5 * _rms(y, gout_ref[...])


def _ffn(h2, g_in, g_out, w_gu, w_down):
    rows = h2.shape[0]
    tm = _pick_tile(rows, (512, 384, 256, 128))
    const = lambda i: (0, 0)
    return pl.pallas_call(
        _ffn_kernel,
        grid=(rows // tm,),
        in_specs=[
            pl.BlockSpec((tm, D_MODEL), lambda i: (i, 0)),
            pl.BlockSpec((1, D_MODEL), const),
            pl.BlockSpec((1, D_MODEL), const),
            pl.BlockSpec((D_MODEL, 2 * D_FF), const, pipeline_mode=pl.Buffered(1)),
            pl.BlockSpec((D_FF, D_MODEL), const, pipeline_mode=pl.Buffered(1)),
        ],
        out_specs=pl.BlockSpec((tm, D_MODEL), lambda i: (i, 0)),
        out_shape=jax.ShapeDtypeStruct((rows, D_MODEL), F32),
        scratch_shapes=[pltpu.VMEM((tm, D_FF), BF16)],
        compiler_params=_params("arbitrary"),
        name="ffn",
    )(h2, g_in, g_out, w_gu, w_down)


def _inproj_kernel(h_ref, g_ref, cos_ref, sin_ref, wconv_ref, smb_ref, vone_ref, w_ref,
                   ya_ref, q_ref, k_ref, v_ref, qi_ref, ki_ref, sm_ref,
                   mq_ref, mk_ref, mv_ref, mo_ref, carry_ref):
    t = pl.program_id(1)
    tm = h_ref.shape[1]

    @pl.when(t == 0)
    def _():
        carry_ref[...] = jnp.zeros_like(carry_ref)

    xn = _rms(h_ref[0], g_ref[...]).astype(BF16)

    def proj(off, width):
        return _dot(xn, w_ref[:, off:off + width])

    cx = proj(OFF_CONV, BRANCH_W)
    cc = proj(OFF_CONV + 2 * BRANCH_W, BRANCH_W)
    z = cc * cx
    row = lax.broadcasted_iota(jnp.int32, z.shape, 0)
    prev1 = jnp.broadcast_to(carry_ref[7:8, :], z.shape)
    prev2 = jnp.broadcast_to(carry_ref[6:7, :], z.shape)
    z1 = jnp.where(row == 0, prev1, pltpu.roll(z, 1, 0))
    z2 = jnp.where(row == 0, prev2, jnp.where(row == 1, prev1, pltpu.roll(z, 2, 0)))
    carry_ref[...] = z[tm - 8:tm, :]
    conv = wconv_ref[0:1, :] * z2 + wconv_ref[1:2, :] * z1 + wconv_ref[2:3, :] * z
    cb = proj(OFF_CONV + BRANCH_W, BRANCH_W)
    ya_ref[0] = (cb * conv).astype(BF16)

    cos = cos_ref[...]
    sin = sin_ref[...]

    def rope_to(out_ref, off, off_rot, width, scale, transposed):
        for c in range(width // LANE):
            x = proj(off + c * LANE, LANE)
            xr = proj(off_rot + c * LANE, LANE)
            y = x * cos + xr * sin
            if scale != 1.0:
                y = y * scale
            if transposed:
                out_ref[0, c * LANE:(c + 1) * LANE, :] = y.T.astype(BF16)
            else:
                out_ref[0, :, c * LANE:(c + 1) * LANE] = y.astype(BF16)

    rope_to(q_ref, OFF_Q, OFF_QR, SEG_Q, Q_SCALE, False)
    rope_to(k_ref, OFF_K, OFF_KR, SEG_K, 1.0, True)
    rope_to(qi_ref, OFF_QI, OFF_QIR, SEG_QI, 1.0, False)
    rope_to(ki_ref, OFF_KI, OFF_KIR, SEG_KI, 1.0, True)
    v_ref[0] = (proj(OFF_V, SEG_V) + vone_ref[...]).astype(BF16)

    raw = proj(OFF_SM, SEG_SM) + smb_ref[...]
    col = lax.broadcasted_iota(jnp.int32, raw.shape, 1)
    logsig = jnp.minimum(raw, 0.0) - jnp.log(1.0 + jnp.exp(-jnp.abs(raw)))
    idx_scale = (IDX_HEADS ** -0.5) * (HEAD_DIM ** -0.5)
    sm_ref[0] = jnp.where(col < IDX_HEADS, raw * idx_scale,
                          jnp.where(col < 2 * IDX_HEADS, raw,
                                    jnp.where(col < 3 * IDX_HEADS, logsig, 0.0)))

    mq_ref[0] = proj(OFF_ML, BRANCH_W).astype(BF16)
    mk_ref[0] = (proj(OFF_ML + BRANCH_W, BRANCH_W) * (ML_DIM ** -0.5)).astype(BF16)
    mv_ref[0] = proj(OFF_ML + 2 * BRANCH_W, BRANCH_W).astype(BF16)
    mo_ref[0] = proj(OFF_ML + 3 * BRANCH_W, BRANCH_W)


def _inproj(h3, g, cos, sin, wconv, smb, vone, w_packed):
    B, TP, _ = h3.shape
    tm = _pick_tile(TP, (384, 256, 128))
    const = lambda b, t: (0, 0)
    tok = lambda b, t: (b, t, 0)

    def out(width, dtype):
        return (pl.BlockSpec((1, tm, width), tok), jax.ShapeDtypeStruct((B, TP, width), dtype))

    def out_t(width, dtype):
        return (pl.BlockSpec((1, width, tm), lambda b, t: (b, 0, t)),
                jax.ShapeDtypeStruct((B, width, TP), dtype))

    outs = [out(BRANCH_W, BF16), out(SEG_Q, BF16), out_t(SEG_K, BF16), out(SEG_V, BF16),
            out(SEG_QI, BF16), out_t(SEG_KI, BF16), out(SEG_SM, F32),
            out(BRANCH_W, BF16), out(BRANCH_W, BF16), out(BRANCH_W, BF16), out(BRANCH_W, F32)]
    return pl.pallas_call(
        _inproj_kernel,
        grid=(B, TP // tm),
        in_specs=[
            pl.BlockSpec((1, tm, D_MODEL), tok),
            pl.BlockSpec((1, D_MODEL), const),
            pl.BlockSpec((tm, LANE), lambda b, t: (t, 0)),
            pl.BlockSpec((tm, LANE), lambda b, t: (t, 0)),
            pl.BlockSpec((8, BRANCH_W), const),
            pl.BlockSpec((1, SEG_SM), const),
            pl.BlockSpec((1, SEG_V), const),
            pl.BlockSpec((D_MODEL, W_PACKED), const, pipeline_mode=pl.Buffered(1)),
        ],
        out_specs=[o[0] for o in outs],
        out_shape=[o[1] for o in outs],
        scratch_shapes=[pltpu.VMEM((8, BRANCH_W), F32)],
        compiler_params=_params("arbitrary", "arbitrary"),
        name="inproj",
    )(h3, g, cos, sin, wconv, smb, vone, w_packed)


def _score_key(sc):
    bits = lax.bitcast_convert_type(sc, jnp.int32)
    return jnp.where(bits < 0, INT_MIN - bits, bits)


def _attn_kernel(q_ref, qi_ref, sm_ref, k_ref, v_ref, ki_ref, o_ref,
                 keys_ref, wb_ref, acc_ref, m_ref, thr_ref, q4_ref, qi4_ref, *, topk, ub):
    i = pl.program_id(1)
    blk = LANE
    uw = ub * blk
    nunits = i + 1
    row = lax.broadcasted_iota(jnp.int32, (uw, blk), 0)
    col = lax.broadcasted_iota(jnp.int32, (uw, blk), 1)

    def unit(u):
        return pl.ds(pl.multiple_of(u * uw, blk), uw)

    def block(u, cb):
        return pl.ds(pl.multiple_of(u * uw + cb * blk, blk), blk)

    for h in range(IDX_HEADS):
        wb_ref[h] = jnp.broadcast_to(sm_ref[0, :, h:h + 1], (uw, blk))
        qi4_ref[h * uw:(h + 1) * uw, :] = qi_ref[0, :, h * LANE:(h + 1) * LANE]
    qpos = i * uw + row

    def p1(u, c):
        rel = _dot(qi4_ref[...], ki_ref[0, :, unit(u)])
        for cb in range(ub):
            lanes = slice(cb * blk, (cb + 1) * blk)
            sc = wb_ref[0] * jnp.maximum(rel[0:uw, lanes], 0.0)
            for h in range(1, IDX_HEADS):
                sc = sc + wb_ref[h] * jnp.maximum(rel[h * uw:(h + 1) * uw, lanes], 0.0)
            kpos = u * uw + cb * blk + col
            keys_ref[:, block(u, cb)] = jnp.where(kpos <= qpos, _score_key(sc), INT_MIN)
        return c

    lax.fori_loop(0, nunits, p1, 0)

    rows_of = [slice(rb * blk, (rb + 1) * blk) for rb in range(ub)]
    col1 = lax.broadcasted_iota(jnp.int32, (blk, blk), 1)

    def lane_counts(rb, pred):
        def body(u, acc):
            for cb in range(ub):
                hit = pred(keys_ref[rows_of[rb], block(u, cb)], u * uw + cb * blk)
                acc = acc + jnp.where(hit, 1.0, 0.0)
            return acc
        return lax.fori_loop(0, nunits, body, jnp.zeros((blk, blk), F32))

    def count_where(rb, pred):
        return jnp.sum(lane_counts(rb, pred), axis=-1, keepdims=True)

    def ge(cand):
        cand_b = jnp.broadcast_to(cand, (blk, blk))
        return lambda kb, base: kb >= cand_b

    def count_ge(rb, cand):
        return count_where(rb, ge(cand))

    kf = float(topk)
    zero = jnp.zeros((blk, 1), jnp.int32)
    r0 = tuple(jnp.where(count_ge(rb, zero) >= kf, zero, INT_MIN) for rb in range(ub))

    def bit_step(t, rs):
        bit = jnp.left_shift(jnp.int32(1), 30 - t)
        cands = [rs[rb] | bit for rb in range(ub)]
        accs = [lane_counts(rb, ge(cands[rb])) for rb in range(ub)]
        return tuple(jnp.where(jnp.sum(accs[rb], axis=-1, keepdims=True) >= kf, cands[rb], rs[rb])
                     for rb in range(ub))

    rs = lax.fori_loop(0, 31, bit_step, r0)
    ties = [jnp.logical_and(rs[rb] > INT_MIN, count_ge(rb, rs[rb]) > kf) for rb in range(ub)]
    any_tie = jnp.max(jnp.where(ties[0], 1.0, 0.0))
    for rb in range(1, ub):
        any_tie = jnp.maximum(any_tie, jnp.max(jnp.where(ties[rb], 1.0, 0.0)))

    @pl.when(any_tie > 0.0)
    def _():
        for rb in range(ub):
            r_b = jnp.broadcast_to(rs[rb], (blk, blk))
            tie_b = jnp.broadcast_to(ties[rb], (blk, blk))
            need = kf - count_ge(rb, rs[rb] + 1)

            def pos_step(t, pos):
                cand = pos + jnp.left_shift(jnp.int32(1), 12 - t)
                cand_b = jnp.broadcast_to(cand, (blk, blk))
                cnt = count_where(rb, lambda kb, base: jnp.logical_and(kb == r_b, col1 + base < cand_b))
                return jnp.where(cnt < need, cand, pos)

            pos_b = jnp.broadcast_to(lax.fori_loop(0, 13, pos_step, zero), (blk, blk))

            def fix(u, c):
                for cb in range(ub):
                    kb = keys_ref[rows_of[rb], block(u, cb)]
                    late = col1 + (u * uw + cb * blk) > pos_b
                    drop = jnp.logical_and(jnp.logical_and(kb == r_b, late), tie_b)
                    keys_ref[rows_of[rb], block(u, cb)] = jnp.where(drop, r_b - 1, kb)
                return c

            lax.fori_loop(0, nunits, fix, 0)

    for rb in range(ub):
        thr_ref[rows_of[rb], :] = jnp.broadcast_to(jnp.maximum(rs[rb], INT_MIN + 1), (blk, blk))
    thr_b = thr_ref[...]

    m_ref[...] = jnp.full(m_ref.shape, NEG_BIG, F32)
    acc_ref[...] = jnp.zeros(acc_ref.shape, F32)
    for g in range(ATT_KV_HEADS):
        for rr in range(ATT_REP):
            h = g * ATT_REP + rr
            q4_ref[g, rr * uw:(rr + 1) * uw, :] = q_ref[0, :, h * LANE:(h + 1) * LANE]

    def p3(u, c):
        bias = [jnp.where(keys_ref[:, block(u, cb)] >= thr_b, 0.0, NEG_BIG) for cb in range(ub)]
        for g in range(ATT_KV_HEADS):
            s4 = _dot(q4_ref[g], k_ref[0, g * LANE:(g + 1) * LANE, unit(u)])
            ps, alphas = [], []
            for rr in range(ATT_REP):
                h = g * ATT_REP + rr
                s = [s4[rr * uw:(rr + 1) * uw, cb * blk:(cb + 1) * blk] + bias[cb] for cb in range(ub)]
                smax = s[0]
                for cb in range(1, ub):
                    smax = jnp.maximum(smax, s[cb])
                m_old = m_ref[h]
                m_new = jnp.maximum(m_old, jnp.max(smax, axis=-1, keepdims=True))
                ps.append(jnp.concatenate([jnp.exp2(sc - m_new).astype(BF16) for sc in s], axis=1))
                alphas.append(jnp.exp2(m_old - m_new))
                m_ref[h] = m_new
            for par in range(2):
                pv = _dot(jnp.concatenate([ps[par], ps[par + 2]], axis=0),
                          v_ref[0, unit(u), (2 * g + par) * LANE:(2 * g + par + 1) * LANE])
                for n, rr in enumerate((par, par + 2)):
                    h = g * ATT_REP + rr
                    acc_ref[h] = alphas[rr] * acc_ref[h] + pv[n * uw:(n + 1) * uw]
        return c

    lax.fori_loop(0, nunits, p3, 0)

    for pr in range(ATT_HEADS // 2):
        ae = acc_ref[2 * pr]
        ao = acc_ref[2 * pr + 1]
        oe = ae / ae[:, HEAD_DIM:HEAD_DIM + 1]
        oo = ao / ao[:, 0:1]
        o_ref[0, :, pr * LANE:(pr + 1) * LANE] = jnp.where(col < HEAD_DIM, oe, oo).astype(BF16)


def _attention(q, qi, sm, k, v, ki, topk):
    B, TP, _ = q.shape
    nblk = TP // LANE
    ub = _pick_tile(nblk, (3, 2, 1))
    uw = ub * LANE
    assert TP <= 2 ** 13
    qmap = lambda b, i: (b, i, 0)
    kmap = lambda b, i: (b, 0, 0)
    return pl.pallas_call(
        functools.partial(_attn_kernel, topk=topk, ub=ub),
        grid=(B, TP // uw),
        in_specs=[
            pl.BlockSpec((1, uw, SEG_Q), qmap),
            pl.BlockSpec((1, uw, SEG_QI), qmap),
            pl.BlockSpec((1, uw, SEG_SM), qmap),
            pl.BlockSpec((1, SEG_K, TP), kmap),
            pl.BlockSpec((1, TP, SEG_V), kmap),
            pl.BlockSpec((1, SEG_KI, TP), kmap),
        ],
        out_specs=pl.BlockSpec((1, uw, BRANCH_W), qmap),
        out_shape=jax.ShapeDtypeStruct((B, TP, BRANCH_W), BF16),
        scratch_shapes=[
            pltpu.VMEM((uw, TP), jnp.int32),
            pltpu.VMEM((IDX_HEADS, uw, LANE), F32),
            pltpu.VMEM((ATT_HEADS, uw, LANE), F32),
            pltpu.VMEM((ATT_HEADS, uw, LANE), F32),
            pltpu.VMEM((uw, LANE), jnp.int32),
            pltpu.VMEM((ATT_KV_HEADS, ATT_REP * uw, LANE), BF16),
            pltpu.VMEM((IDX_HEADS * uw, LANE), BF16),
        ],
        compiler_params=_params("arbitrary", "arbitrary"),
        name="attn",
    )(q, qi, sm, k, v, ki)


def _mlstm_kernel(q_ref, k_ref, v_ref, mo_ref, sm_ref, g_ref, o_ref, c_ref, m_ref):
    L = LANE

    @pl.when(pl.program_id(1) == 0)
    def _():
        c_ref[...] = jnp.zeros_like(c_ref)
        m_ref[...] = jnp.zeros_like(m_ref)

    row = lax.broadcasted_iota(jnp.int32, (L, L), 0)
    col = lax.broadcasted_iota(jnp.int32, (L, L), 1)
    tril = col <= row
    sm = sm_ref[0]
    cum = jnp.dot(jnp.where(tril, 1.0, 0.0), sm, precision=lax.Precision.HIGHEST,
                  preferred_element_type=F32)
    sm_t = sm.T
    cum_t = cum.T
    ones_col = jnp.where(col == 0, 1.0, 0.0).astype(BF16)

    for h in range(ML_HEADS):
        lanes = slice(h * ML_DIM, (h + 1) * ML_DIM)
        ci, cf = IDX_HEADS + h, 2 * IDX_HEADS + h
        b_col = cum[:, cf:cf + 1]
        b_row = cum_t[cf:cf + 1, :]
        b_last = cum[L - 1:L, cf:cf + 1]
        i_col = sm[:, ci:ci + 1]
        i_row = sm_t[ci:ci + 1, :]
        m_old = m_ref[h][0:1, 0:1]
        qh = q_ref[0, :, lanes]
        kh = k_ref[0, :, lanes]
        v_aug = jnp.concatenate([v_ref[0, :, lanes], ones_col], axis=1)

        dm = jnp.where(tril, b_col - b_row + i_row, -jnp.inf)
        a = b_col + m_old
        mq = jnp.maximum(a, jnp.max(dm, axis=-1, keepdims=True))
        inter = jnp.exp(a - mq)
        wqk = jnp.exp(dm - mq) * _dot_nt(qh, kh)
        c_old = c_ref[h]
        res = inter * _dot(qh, c_old.astype(BF16)) + _dot(wqk.astype(BF16), v_aug)
        den = res[:, ML_DIM:ML_DIM + 1]
        hc = res[:, 0:ML_DIM] / jnp.maximum(jnp.abs(den), jnp.exp(-mq))

        g_col = b_last - b_col + i_col
        m_new = jnp.maximum(b_last + m_old, jnp.max(g_col, axis=0, keepdims=True))
        decay = jnp.exp(b_last + m_old - m_new)
        wk = jnp.exp(g_col - m_new) * kh.astype(F32)
        c_ref[h] = decay * c_old + _dot(wk.T.astype(BF16), v_aug)
        m_ref[h] = jnp.broadcast_to(m_new, m_ref.shape[1:])

        hn = hc * lax.rsqrt(jnp.mean(hc * hc, axis=-1, keepdims=True) + EPS) * g_ref[:, lanes]
        o_ref[0, :, lanes] = (hn * jax.nn.sigmoid(mo_ref[0, :, lanes])).astype(BF16)


def _mlstm(mq, mk, mv, mo, sm, g):
    B, TP, _ = mq.shape
    tok = lambda b, c: (b, c, 0)
    return pl.pallas_call(
        _mlstm_kernel,
        grid=(B, TP // LANE),
        in_specs=[
            pl.BlockSpec((1, LANE, BRANCH_W), tok),
            pl.BlockSpec((1, LANE, BRANCH_W), tok),
            pl.BlockSpec((1, LANE, BRANCH_W), tok),
            pl.BlockSpec((1, LANE, BRANCH_W), tok),
            pl.BlockSpec((1, LANE, SEG_SM), tok),
            pl.BlockSpec((1, BRANCH_W), lambda b, c: (0, 0)),
        ],
        out_specs=pl.BlockSpec((1, LANE, BRANCH_W), tok),
        out_shape=jax.ShapeDtypeStruct((B, TP, BRANCH_W), BF16),
        scratch_shapes=[
            pltpu.VMEM((ML_HEADS, ML_DIM, 2 * ML_DIM), F32),
            pltpu.VMEM((ML_HEADS, 8, LANE), F32),
        ],
        compiler_params=_params("arbitrary", "arbitrary"),
        name="mlstm",
    )(mq, mk, mv, mo, sm, g)


def _merge_kernel(h_ref, ya_ref, yb_ref, yc_ref, gin_ref, gout_ref, wgt_ref, wbr_ref, wo_ref, o_ref):
    x = h_ref[...]
    xn = _rms(x, gin_ref[...]).astype(BF16)
    merged = None
    for n, y_ref in enumerate((ya_ref, yb_ref, yc_ref)):
        gate = jax.nn.sigmoid(_dot(xn, wgt_ref[:, n * D_MODEL:(n + 1) * D_MODEL]))
        term = gate * _dot(y_ref[...], wbr_ref[n])
        merged = term if merged is None else merged + term
    y = _dot(merged.astype(BF16), wo_ref[...])
    o_ref[...] = x + _rms(y, gout_ref[...])


def _merge(h2, ya, yb, yc, g_in, g_out, w_gt, w_br, w_o):
    rows = h2.shape[0]
    tm = _pick_tile(rows, (512, 384, 256, 128))
    const2 = lambda i: (0, 0)
    tok = lambda i: (i, 0)
    return pl.pallas_call(
        _merge_kernel,
        grid=(rows // tm,),
        in_specs=[
            pl.BlockSpec((tm, D_MODEL), tok),
            pl.BlockSpec((tm, BRANCH_W), tok),
            pl.BlockSpec((tm, BRANCH_W), tok),
            pl.BlockSpec((tm, BRANCH_W), tok),
            pl.BlockSpec((1, D_MODEL), const2),
            pl.BlockSpec((1, D_MODEL), const2),
            pl.BlockSpec((D_MODEL, N_BRANCH * D_MODEL), const2, pipeline_mode=pl.Buffered(1)),
            pl.BlockSpec((N_BRANCH, BRANCH_W, D_MODEL), lambda i: (0, 0, 0), pipeline_mode=pl.Buffered(1)),
            pl.BlockSpec((D_MODEL, D_MODEL), const2, pipeline_mode=pl.Buffered(1)),
        ],
        out_specs=pl.BlockSpec((tm, D_MODEL), tok),
        out_shape=jax.ShapeDtypeStruct((rows, D_MODEL), F32),
        compiler_params=_params("arbitrary"),
        name="merge",
    )(h2, ya, yb, yc, g_in, g_out, w_gt, w_br, w_o)


def _rot_cols(w):
    d, n = w.shape
    w = w.reshape(d, n // HEAD_DIM, 2, HEAD_DIM // 2)
    return jnp.stack([-w[:, :, 1], w[:, :, 0]], axis=2).reshape(d, n)


def _pad_heads(w):
    d, n = w.shape
    w = w.reshape(d, n // HEAD_DIM, HEAD_DIM)
    return jnp.pad(w, ((0, 0), (0, 0), (0, LANE - HEAD_DIM))).reshape(d, -1)


def _pack_w_in(w_in):
    sizes = (BRANCH_W, BRANCH_W, BRANCH_W, ATT_HEADS * HEAD_DIM, ATT_KV_HEADS * HEAD_DIM,
             ATT_KV_HEADS * HEAD_DIM, IDX_HEADS * HEAD_DIM, HEAD_DIM, IDX_HEADS,
             BRANCH_W, BRANCH_W, BRANCH_W, BRANCH_W, ML_HEADS, ML_HEADS, N_BRANCH * D_MODEL)
    offs = np.cumsum(sizes)[:-1].tolist()
    (cx, cb, cc, q, k, v, qi, ki, wi, mq, mk, mv, mo, mi, mf, gt) = jnp.split(w_in, offs, axis=1)
    d = w_in.shape[0]
    zeros64 = jnp.zeros((d, HEAD_DIM), w_in.dtype)
    v_parts = []
    for g in range(ATT_KV_HEADS):
        vg = v[:, g * HEAD_DIM:(g + 1) * HEAD_DIM]
        v_parts += [vg, zeros64, zeros64, vg]
    small = jnp.concatenate([wi, mi, mf, jnp.zeros((d, SEG_SM - 3 * IDX_HEADS), w_in.dtype)], axis=1)
    packed = jnp.concatenate([
        cx, cb, cc,
        _pad_heads(q), _pad_heads(_rot_cols(q)),
        _pad_heads(k), _pad_heads(_rot_cols(k)),
        *v_parts,
        _pad_heads(qi), _pad_heads(_rot_cols(qi)),
        _pad_heads(ki), _pad_heads(_rot_cols(ki)),
        small, mq, mk, mv, mo], axis=1)
    assert packed.shape[1] == W_PACKED
    return packed.astype(BF16), gt.astype(BF16)


def _rope_tables(tp):
    inv = 1.0 / (ROPE_THETA ** (jnp.arange(0, HEAD_DIM, 2, dtype=F32) / HEAD_DIM))
    ang = jnp.arange(tp, dtype=F32)[:, None] * inv[None, :]
    reps = 2 * LANE // HEAD_DIM
    return jnp.tile(jnp.cos(ang), (1, reps)), jnp.tile(jnp.sin(ang), (1, reps))


def kernel(x, meta, norm_g, w_ffn_gu, w_ffn_down, w_in, w_conv, b_igate, b_fgate, mh_norm_g, w_branch, w_out):
    B, S, _ = x.shape
    depth = norm_g.shape[0]
    T = N_META + S
    TP = -(-T // SEQ_ALIGN) * SEQ_ALIGN
    topk = min(IDX_TOPK_MAX, S // 4)

    h = jnp.concatenate([jnp.broadcast_to(meta.astype(x.dtype)[None], (B, N_META, D_MODEL)), x,
                         jnp.zeros((B, TP - T, D_MODEL), x.dtype)], axis=1)
    cos, sin = _rope_tables(TP)
    vone = np.zeros((1, SEG_V), np.float32)
    for g in range(ATT_KV_HEADS):
        vone[0, (2 * g) * LANE + HEAD_DIM] = 1.0
        vone[0, (2 * g + 1) * LANE] = 1.0
    vone = jnp.asarray(vone)

    h2 = h.reshape(B * TP, D_MODEL)
    for l in range(depth):
        g = norm_g[l][:, None, :]
        w_packed, w_gt = _pack_w_in(w_in[l])
        smb = jnp.concatenate([jnp.zeros((IDX_HEADS,), F32), b_igate[l], b_fgate[l],
                               jnp.zeros((SEG_SM - 3 * IDX_HEADS,), F32)])[None, :]
        wconv = jnp.pad(w_conv[l], ((0, 8 - CONV_K), (0, 0)))

        h2 = _ffn(h2, g[0], g[1], w_ffn_gu[l, 0].astype(BF16), w_ffn_down[l, 0].astype(BF16))
        (ya, q, k, v, qi, ki, sm, mq, mk, mv, mo) = _inproj(
            h2.reshape(B, TP, D_MODEL), g[2], cos, sin, wconv, smb, vone, w_packed)
        yb = _attention(q, qi, sm, k, v, ki, topk)
        yc = _mlstm(mq, mk, mv, mo, sm, mh_norm_g[l][None, :])
        flat = lambda a: a.reshape(B * TP, BRANCH_W)
        h2 = _merge(h2, flat(ya), flat(yb), flat(yc), g[2], g[3], w_gt,
                    w_branch[l].astype(BF16), w_out[l].astype(BF16))
        h2 = _ffn(h2, g[4], g[5], w_ffn_gu[l, 1].astype(BF16), w_ffn_down[l, 1].astype(BF16))
    return h2.reshape(B, TP, D_MODEL)[:, N_META:T]
```

```python
import functools

import numpy as np
import jax
import jax.numpy as jnp
from jax import lax
from jax.experimental import pallas as pl
from jax.experimental.pallas import tpu as pltpu

F32 = jnp.float32
BF16 = jnp.bfloat16

D_MODEL = 1024
N_META = 16
EPS = 1e-6
D_FF = 11 * D_MODEL // 4
BRANCH_W = D_MODEL // 2
N_BRANCH = 3
CONV_K = 3
HEAD_DIM = 64
ATT_HEADS = 8
ATT_KV_HEADS = 2
ATT_REP = ATT_HEADS // ATT_KV_HEADS
IDX_HEADS = 4
IDX_TOPK_MAX = 256
ML_HEADS = 4
ML_DIM = 128
ROPE_THETA = 10000.0

LANE = 128
FF_CHUNK = 256
SEQ_ALIGN = LANE
VMEM_LIMIT = 56 * 1024 * 1024
Q_SCALE = HEAD_DIM ** -0.5 * 1.4426950408889634
INT_MIN = -2 ** 31
NEG_BIG = -1e30

SEG_CONV = 3 * BRANCH_W
SEG_Q = ATT_HEADS * LANE
SEG_K = ATT_KV_HEADS * LANE
SEG_V = 2 * ATT_KV_HEADS * LANE
SEG_QI = IDX_HEADS * LANE
SEG_KI = LANE
SEG_SM = LANE
SEG_ML = 4 * BRANCH_W
OFF_CONV = 0
OFF_Q = OFF_CONV + SEG_CONV
OFF_QR = OFF_Q + SEG_Q
OFF_K = OFF_QR + SEG_Q
OFF_KR = OFF_K + SEG_K
OFF_V = OFF_KR + SEG_K
OFF_QI = OFF_V + SEG_V
OFF_QIR = OFF_QI + SEG_QI
OFF_KI = OFF_QIR + SEG_QI
OFF_KIR = OFF_KI + SEG_KI
OFF_SM = OFF_KIR + SEG_KI
OFF_ML = OFF_SM + SEG_SM
W_PACKED = OFF_ML + SEG_ML


def _pick_tile(n, candidates):
    for c in candidates:
        if n % c == 0:
            return c
    raise ValueError(f"no tile for {n}")


def _rms(x, g):
    return x * lax.rsqrt(jnp.mean(x * x, axis=-1, keepdims=True) + EPS) * g


def _dot(a, b):
    return jnp.dot(a, b, preferred_element_type=F32)


def _dot_nt(a, b):
    return lax.dot_general(a, b, (((1,), (1,)), ((), ())), preferred_element_type=F32)


def _params(*sem):
    return pltpu.CompilerParams(dimension_semantics=sem, vmem_limit_bytes=VMEM_LIMIT)


def _ffn_kernel(h_ref, gin_ref, gout_ref, wgu_ref, wd_ref, o_ref, act_ref):
    x = h_ref[...]
    xn = _rms(x, gin_ref[...]).astype(BF16)
    for c in range(D_FF // FF_CHUNK):
        lo = c * FF_CHUNK
        gate = _dot(xn, wgu_ref[:, lo:lo + FF_CHUNK])
        up = _dot(xn, wgu_ref[:, D_FF + lo:D_FF + lo + FF_CHUNK])
        act_ref[:, lo:lo + FF_CHUNK] = (gate * jax.nn.sigmoid(gate) * up).astype(BF16)
    y = _dot(act_ref[...], wd_ref[...])
    o_ref[...] = x + 0.5 * _rms(y, gout_ref[...])


def _ffn(h2, g_in, g_out, w_gu, w_down):
    rows = h2.shape[0]
    tm = _pick_tile(rows, (512, 384, 256, 128))
    const = lambda i: (0, 0)
    return pl.pallas_call(
        _ffn_kernel,
        grid=(rows // tm,),
        in_specs=[
            pl.BlockSpec((tm, D_MODEL), lambda i: (i, 0)),
            pl.BlockSpec((1, D_MODEL), const),
            pl.BlockSpec((1, D_MODEL), const),
            pl.BlockSpec((D_MODEL, 2 * D_FF), const, pipeline_mode=pl.Buffered(1)),
            pl.BlockSpec((D_FF, D_MODEL), const, pipeline_mode=pl.Buffered(1)),
        ],
        out_specs=pl.BlockSpec((tm, D_MODEL), lambda i: (i, 0)),
        out_shape=jax.ShapeDtypeStruct((rows, D_MODEL), F32),
        scratch_shapes=[pltpu.VMEM((tm, D_FF), BF16)],
        compiler_params=_params("arbitrary"),
        name="ffn",
    )(h2, g_in, g_out, w_gu, w_down)


def _inproj_kernel(h_ref, g_ref, cos_ref, sin_ref, wconv_ref, smb_ref, vone_ref, w_ref,
                   ya_ref, q_ref, k_ref, v_ref, qi_ref, ki_ref, sm_ref,
                   mq_ref, mk_ref, mv_ref, mo_ref, carry_ref):
    t = pl.program_id(1)
    tm = h_ref.shape[1]

    @pl.when(t == 0)
    def _():
        carry_ref[...] = jnp.zeros_like(carry_ref)

    xn = _rms(h_ref[0], g_ref[...]).astype(BF16)

    def proj(off, width):
        return _dot(xn, w_ref[:, off:off + width])

    cx = proj(OFF_CONV, BRANCH_W)
    cc = proj(OFF_CONV + 2 * BRANCH_W, BRANCH_W)
    z = cc * cx
    row = lax.broadcasted_iota(jnp.int32, z.shape, 0)
    prev1 = jnp.broadcast_to(carry_ref[7:8, :], z.shape)
    prev2 = jnp.broadcast_to(carry_ref[6:7, :], z.shape)
    z1 = jnp.where(row == 0, prev1, pltpu.roll(z, 1, 0))
    z2 = jnp.where(row == 0, prev2, jnp.where(row == 1, prev1, pltpu.roll(z, 2, 0)))
    carry_ref[...] = z[tm - 8:tm, :]
    conv = wconv_ref[0:1, :] * z2 + wconv_ref[1:2, :] * z1 + wconv_ref[2:3, :] * z
    cb = proj(OFF_CONV + BRANCH_W, BRANCH_W)
    ya_ref[0] = (cb * conv).astype(BF16)

    cos = cos_ref[...]
    sin = sin_ref[...]

    def rope_to(out_ref, off, off_rot, width, scale, transposed):
        step = min(width, FF_CHUNK)
        for c0 in range(0, width, step):
            x2 = proj(off + c0, step)
            xr2 = proj(off_rot + c0, step)
            for c in range(c0 // LANE, (c0 + step) // LANE):
                lanes = slice(c * LANE - c0, (c + 1) * LANE - c0)
                y = x2[:, lanes] * cos + xr2[:, lanes] * sin
                if scale != 1.0:
                    y = y * scale
                if transposed:
                    out_ref[0, c * LANE:(c + 1) * LANE, :] = y.T.astype(BF16)
                else:
                    out_ref[0, :, c * LANE:(c + 1) * LANE] = y.astype(BF16)

    rope_to(q_ref, OFF_Q, OFF_QR, SEG_Q, Q_SCALE, False)
    rope_to(k_ref, OFF_K, OFF_KR, SEG_K, 1.0, True)
    rope_to(qi_ref, OFF_QI, OFF_QIR, SEG_QI, 1.0, False)
    rope_to(ki_ref, OFF_KI, OFF_KIR, SEG_KI, 1.0, True)
    v_ref[0] = (proj(OFF_V, SEG_V) + vone_ref[...]).astype(BF16)

    raw = proj(OFF_SM, SEG_SM) + smb_ref[...]
    col = lax.broadcasted_iota(jnp.int32, raw.shape, 1)
    logsig = jnp.minimum(raw, 0.0) - jnp.log(1.0 + jnp.exp(-jnp.abs(raw)))
    idx_scale = (IDX_HEADS ** -0.5) * (HEAD_DIM ** -0.5)
    sm_ref[0] = jnp.where(col < IDX_HEADS, raw * idx_scale,
                          jnp.where(col < 2 * IDX_HEADS, raw,
                                    jnp.where(col < 3 * IDX_HEADS, logsig, 0.0)))

    mq_ref[0] = proj(OFF_ML, BRANCH_W).astype(BF16)
    mk_ref[0] = (proj(OFF_ML + BRANCH_W, BRANCH_W) * (ML_DIM ** -0.5)).astype(BF16)
    mv_ref[0] = proj(OFF_ML + 2 * BRANCH_W, BRANCH_W).astype(BF16)
    mo_ref[0] = proj(OFF_ML + 3 * BRANCH_W, BRANCH_W)


def _inproj(h3, g, cos, sin, wconv, smb, vone, w_packed):
    B, TP, _ = h3.shape
    tm = _pick_tile(TP, (384, 256, 128))
    const = lambda b, t: (0, 0)
    tok = lambda b, t: (b, t, 0)

    def out(width, dtype):
        return (pl.BlockSpec((1, tm, width), tok), jax.ShapeDtypeStruct((B, TP, width), dtype))

    def out_t(width, dtype):
        return (pl.BlockSpec((1, width, tm), lambda b, t: (b, 0, t)),
                jax.ShapeDtypeStruct((B, width, TP), dtype))

    outs = [out(BRANCH_W, BF16), out(SEG_Q, BF16), out_t(SEG_K, BF16), out(SEG_V, BF16),
            out(SEG_QI, BF16), out_t(SEG_KI, BF16), out(SEG_SM, F32),
            out(BRANCH_W, BF16), out(BRANCH_W, BF16), out(BRANCH_W, BF16), out(BRANCH_W, F32)]
    return pl.pallas_call(
        _inproj_kernel,
        grid=(B, TP // tm),
        in_specs=[
            pl.BlockSpec((1, tm, D_MODEL), tok),
            pl.BlockSpec((1, D_MODEL), const),
            pl.BlockSpec((tm, LANE), lambda b, t: (t, 0)),
            pl.BlockSpec((tm, LANE), lambda b, t: (t, 0)),
            pl.BlockSpec((8, BRANCH_W), const),
            pl.BlockSpec((1, SEG_SM), const),
            pl.BlockSpec((1, SEG_V), const),
            pl.BlockSpec((D_MODEL, W_PACKED), const, pipeline_mode=pl.Buffered(1)),
        ],
        out_specs=[o[0] for o in outs],
        out_shape=[o[1] for o in outs],
        scratch_shapes=[pltpu.VMEM((8, BRANCH_W), F32)],
        compiler_params=_params("arbitrary", "arbitrary"),
        name="inproj",
    )(h3, g, cos, sin, wconv, smb, vone, w_packed)


def _score_key(sc):
    bits = lax.bitcast_convert_type(sc, jnp.int32)
    return jnp.where(bits < 0, INT_MIN - bits, bits)


def _attn_kernel(q_ref, qi_ref, sm_ref, k_ref, v_ref, ki_ref, o_ref,
                 keys_ref, wb_ref, acc_ref, m_ref, thr_ref, q4_ref, qi4_ref, *, topk, ub):
    i = pl.program_id(1)
    blk = LANE
    uw = ub * blk
    nunits = i + 1
    row = lax.broadcasted_iota(jnp.int32, (uw, blk), 0)
    col = lax.broadcasted_iota(jnp.int32, (uw, blk), 1)

    def unit(u):
        return pl.ds(pl.multiple_of(u * uw, blk), uw)

    def block(u, cb):
        return pl.ds(pl.multiple_of(u * uw + cb * blk, blk), blk)

    for h in range(IDX_HEADS):
        wb_ref[h] = jnp.broadcast_to(sm_ref[0, :, h:h + 1], (uw, blk))
        qi4_ref[h * uw:(h + 1) * uw, :] = qi_ref[0, :, h * LANE:(h + 1) * LANE]
    qpos = i * uw + row

    def p1(u, c):
        rel = _dot(qi4_ref[...], ki_ref[0, :, unit(u)])
        for cb in range(ub):
            lanes = slice(cb * blk, (cb + 1) * blk)
            sc = wb_ref[0] * jnp.maximum(rel[0:uw, lanes], 0.0)
            for h in range(1, IDX_HEADS):
                sc = sc + wb_ref[h] * jnp.maximum(rel[h * uw:(h + 1) * uw, lanes], 0.0)
            kpos = u * uw + cb * blk + col
            keys_ref[:, block(u, cb)] = jnp.where(kpos <= qpos, _score_key(sc), INT_MIN)
        return c

    lax.fori_loop(0, nunits, p1, 0)

    rows_of = [slice(rb * blk, (rb + 1) * blk) for rb in range(ub)]
    col1 = lax.broadcasted_iota(jnp.int32, (blk, blk), 1)

    def lane_counts(rb, pred):
        def body(u, acc):
            for cb in range(ub):
                hit = pred(keys_ref[rows_of[rb], block(u, cb)], u * uw + cb * blk)
                acc = acc + jnp.where(hit, 1.0, 0.0)
            return acc
        return lax.fori_loop(0, nunits, body, jnp.zeros((blk, blk), F32))

    def count_where(rb, pred):
        return jnp.sum(lane_counts(rb, pred), axis=-1, keepdims=True)

    def ge(cand):
        cand_b = jnp.broadcast_to(cand, (blk, blk))
        return lambda kb, base: kb >= cand_b

    def count_ge(rb, cand):
        return count_where(rb, ge(cand))

    kf = float(topk)
    zero = jnp.zeros((blk, 1), jnp.int32)
    c0 = tuple(count_ge(rb, zero) for rb in range(ub))
    r0 = tuple(jnp.where(c0[rb] >= kf, zero, INT_MIN) for rb in range(ub))

    def bit_step(t, carry):
        rs, cs = carry
        bit = jnp.left_shift(jnp.int32(1), 30 - t)
        cands = [rs[rb] | bit for rb in range(ub)]
        accs = [lane_counts(rb, ge(cands[rb])) for rb in range(ub)]
        cnts = [jnp.sum(accs[rb], axis=-1, keepdims=True) for rb in range(ub)]
        return (tuple(jnp.where(cnts[rb] >= kf, cands[rb], rs[rb]) for rb in range(ub)),
                tuple(jnp.where(cnts[rb] >= kf, cnts[rb], cs[rb]) for rb in range(ub)))

    rs, cs = lax.fori_loop(0, 31, bit_step, (r0, c0))
    ties = [jnp.logical_and(rs[rb] > INT_MIN, cs[rb] > kf) for rb in range(ub)]
    any_tie = jnp.max(jnp.where(ties[0], 1.0, 0.0))
    for rb in range(1, ub):
        any_tie = jnp.maximum(any_tie, jnp.max(jnp.where(ties[rb], 1.0, 0.0)))

    @pl.when(any_tie > 0.0)
    def _():
        for rb in range(ub):
            r_b = jnp.broadcast_to(rs[rb], (blk, blk))
            tie_b = jnp.broadcast_to(ties[rb], (blk, blk))
            need = kf - count_ge(rb, rs[rb] + 1)

            def pos_step(t, pos):
                cand = pos + jnp.left_shift(jnp.int32(1), 12 - t)
                cand_b = jnp.broadcast_to(cand, (blk, blk))
                cnt = count_where(rb, lambda kb, base: jnp.logical_and(kb == r_b, col1 + base < cand_b))
                return jnp.where(cnt < need, cand, pos)

            pos_b = jnp.broadcast_to(lax.fori_loop(0, 13, pos_step, zero), (blk, blk))

            def fix(u, c):
                for cb in range(ub):
                    kb = keys_ref[rows_of[rb], block(u, cb)]
                    late = col1 + (u * uw + cb * blk) > pos_b
                    drop = jnp.logical_and(jnp.logical_and(kb == r_b, late), tie_b)
                    keys_ref[rows_of[rb], block(u, cb)] = jnp.where(drop, r_b - 1, kb)
                return c

            lax.fori_loop(0, nunits, fix, 0)

    for rb in range(ub):
        thr_ref[rows_of[rb], :] = jnp.broadcast_to(jnp.maximum(rs[rb], INT_MIN + 1), (blk, blk))
    thr_b = thr_ref[...]

    m_ref[...] = jnp.full(m_ref.shape, NEG_BIG, F32)
    acc_ref[...] = jnp.zeros(acc_ref.shape, F32)
    for g in range(ATT_KV_HEADS):
        for rr in range(ATT_REP):
            h = g * ATT_REP + rr
            q4_ref[g, rr * uw:(rr + 1) * uw, :] = q_ref[0, :, h * LANE:(h + 1) * LANE]

    def p3(u, c):
        bias = [jnp.where(keys_ref[:, block(u, cb)] >= thr_b, 0.0, NEG_BIG) for cb in range(ub)]
        for g in range(ATT_KV_HEADS):
            s4 = _dot(q4_ref[g], k_ref[0, g * LANE:(g + 1) * LANE, unit(u)])
            ps, alphas = [], []
            for rr in range(ATT_REP):
                h = g * ATT_REP + rr
                s = [s4[rr * uw:(rr + 1) * uw, cb * blk:(cb + 1) * blk] + bias[cb] for cb in range(ub)]
                smax = s[0]
                for cb in range(1, ub):
                    smax = jnp.maximum(smax, s[cb])
                m_old = m_ref[h]
                m_new = jnp.maximum(m_old, jnp.max(smax, axis=-1, keepdims=True))
                ps.append(jnp.concatenate([jnp.exp2(sc - m_new).astype(BF16) for sc in s], axis=1))
                alphas.append(jnp.exp2(m_old - m_new))
                m_ref[h] = m_new
            for par in range(2):
                pv = _dot(jnp.concatenate([ps[par], ps[par + 2]], axis=0),
                          v_ref[0, unit(u), (2 * g + par) * LANE:(2 * g + par + 1) * LANE])
                for n, rr in enumerate((par, par + 2)):
                    h = g * ATT_REP + rr
                    acc_ref[h] = alphas[rr] * acc_ref[h] + pv[n * uw:(n + 1) * uw]
        return c

    lax.fori_loop(0, nunits, p3, 0)

    for pr in range(ATT_HEADS // 2):
        ae = acc_ref[2 * pr]
        ao = acc_ref[2 * pr + 1]
        oe = ae / ae[:, HEAD_DIM:HEAD_DIM + 1]
        oo = ao / ao[:, 0:1]
        o_ref[0, :, pr * LANE:(pr + 1) * LANE] = jnp.where(col < HEAD_DIM, oe, oo).astype(BF16)


def _attention(q, qi, sm, k, v, ki, topk):
    B, TP, _ = q.shape
    nblk = TP // LANE
    ub = _pick_tile(nblk, (3, 2, 1))
    uw = ub * LANE
    assert TP <= 2 ** 13
    qmap = lambda b, i: (b, i, 0)
    kmap = lambda b, i: (b, 0, 0)
    return pl.pallas_call(
        functools.partial(_attn_kernel, topk=topk, ub=ub),
        grid=(B, TP // uw),
        in_specs=[
            pl.BlockSpec((1, uw, SEG_Q), qmap),
            pl.BlockSpec((1, uw, SEG_QI), qmap),
            pl.BlockSpec((1, uw, SEG_SM), qmap),
            pl.BlockSpec((1, SEG_K, TP), kmap),
            pl.BlockSpec((1, TP, SEG_V), kmap),
            pl.BlockSpec((1, SEG_KI, TP), kmap),
        ],
        out_specs=pl.BlockSpec((1, uw, BRANCH_W), qmap),
        out_shape=jax.ShapeDtypeStruct((B, TP, BRANCH_W), BF16),
        scratch_shapes=[
            pltpu.VMEM((uw, TP), jnp.int32),
            pltpu.VMEM((IDX_HEADS, uw, LANE), F32),
            pltpu.VMEM((ATT_HEADS, uw, LANE), F32),
            pltpu.VMEM((ATT_HEADS, uw, LANE), F32),
            pltpu.VMEM((uw, LANE), jnp.int32),
            pltpu.VMEM((ATT_KV_HEADS, ATT_REP * uw, LANE), BF16),
            pltpu.VMEM((IDX_HEADS * uw, LANE), BF16),
        ],
        compiler_params=_params("arbitrary", "arbitrary"),
        name="attn",
    )(q, qi, sm, k, v, ki)


def _mlstm_kernel(q_ref, k_ref, v_ref, mo_ref, sm_ref, g_ref, o_ref, c_ref, m_ref):
    L = LANE

    @pl.when(pl.program_id(1) == 0)
    def _():
        c_ref[...] = jnp.zeros_like(c_ref)
        m_ref[...] = jnp.zeros_like(m_ref)

    row = lax.broadcasted_iota(jnp.int32, (L, L), 0)
    col = lax.broadcasted_iota(jnp.int32, (L, L), 1)
    tril = col <= row
    sm = sm_ref[0]
    cum = jnp.dot(jnp.where(tril, 1.0, 0.0), sm, precision=lax.Precision.HIGHEST,
                  preferred_element_type=F32)
    sm_t = sm.T
    cum_t = cum.T
    ones_col = jnp.where(col == 0, 1.0, 0.0).astype(BF16)

    for h in range(ML_HEADS):
        lanes = slice(h * ML_DIM, (h + 1) * ML_DIM)
        ci, cf = IDX_HEADS + h, 2 * IDX_HEADS + h
        b_col = cum[:, cf:cf + 1]
        b_row = cum_t[cf:cf + 1, :]
        b_last = cum[L - 1:L, cf:cf + 1]
        i_col = sm[:, ci:ci + 1]
        i_row = sm_t[ci:ci + 1, :]
        m_old = m_ref[h][0:1, 0:1]
        qh = q_ref[0, :, lanes]
        kh = k_ref[0, :, lanes]
        v_aug = jnp.concatenate([v_ref[0, :, lanes], ones_col], axis=1)

        dm = jnp.where(tril, b_col - b_row + i_row, -jnp.inf)
        a = b_col + m_old
        mq = jnp.maximum(a, jnp.max(dm, axis=-1, keepdims=True))
        inter = jnp.exp(a - mq)
        wqk = jnp.exp(dm - mq) * _dot_nt(qh, kh)
        c_old = c_ref[h]
        res = inter * _dot(qh, c_old.astype(BF16)) + _dot(wqk.astype(BF16), v_aug)
        den = res[:, ML_DIM:ML_DIM + 1]
        hc = res[:, 0:ML_DIM] / jnp.maximum(jnp.abs(den), jnp.exp(-mq))

        g_col = b_last - b_col + i_col
        m_new = jnp.maximum(b_last + m_old, jnp.max(g_col, axis=0, keepdims=True))
        decay = jnp.exp(b_last + m_old - m_new)
        wk = jnp.exp(g_col - m_new) * kh.astype(F32)
        c_ref[h] = decay * c_old + _dot(wk.T.astype(BF16), v_aug)
        m_ref[h] = jnp.broadcast_to(m_new, m_ref.shape[1:])

        hn = hc * lax.rsqrt(jnp.mean(hc * hc, axis=-1, keepdims=True) + EPS) * g_ref[:, lanes]
        o_ref[0, :, lanes] = (hn * jax.nn.sigmoid(mo_ref[0, :, lanes])).astype(BF16)


def _mlstm(mq, mk, mv, mo, sm, g):
    B, TP, _ = mq.shape
    tok = lambda b, c: (b, c, 0)
    return pl.pallas_call(
        _mlstm_kernel,
        grid=(B, TP // LANE),
        in_specs=[
            pl.BlockSpec((1, LANE, BRANCH_W), tok),
            pl.BlockSpec((1, LANE, BRANCH_W), tok),
            pl.BlockSpec((1, LANE, BRANCH_W), tok),
            pl.BlockSpec((1, LANE, BRANCH_W), tok),
            pl.BlockSpec((1, LANE, SEG_SM), tok),
            pl.BlockSpec((1, BRANCH_W), lambda b, c: (0, 0)),
        ],
        out_specs=pl.BlockSpec((1, LANE, BRANCH_W), tok),
        out_shape=jax.ShapeDtypeStruct((B, TP, BRANCH_W), BF16),
        scratch_shapes=[
            pltpu.VMEM((ML_HEADS, ML_DIM, 2 * ML_DIM), F32),
            pltpu.VMEM((ML_HEADS, 8, LANE), F32),
        ],
        compiler_params=_params("arbitrary", "arbitrary"),
        name="mlstm",
    )(mq, mk, mv, mo, sm, g)


def _merge_kernel(h_ref, ya_ref, yb_ref, yc_ref, gin_ref, gout_ref, wgt_ref, wbr_ref, wo_ref, o_ref):
    x = h_ref[...]
    xn = _rms(x, gin_ref[...]).astype(BF16)
    merged = None
    for n, y_ref in enumerate((ya_ref, yb_ref, yc_ref)):
        gate = jax.nn.sigmoid(_dot(xn, wgt_ref[:, n * D_MODEL:(n + 1) * D_MODEL]))
        term = gate * _dot(y_ref[...], wbr_ref[n])
        merged = term if merged is None else merged + term
    y = _dot(merged.astype(BF16), wo_ref[...])
    o_ref[...] = x + _rms(y, gout_ref[...])


def _merge(h2, ya, yb, yc, g_in, g_out, w_gt, w_br, w_o):
    rows = h2.shape[0]
    tm = _pick_tile(rows, (512, 384, 256, 128))
    const2 = lambda i: (0, 0)
    tok = lambda i: (i, 0)
    return pl.pallas_call(
        _merge_kernel,
        grid=(rows // tm,),
        in_specs=[
            pl.BlockSpec((tm, D_MODEL), tok),
            pl.BlockSpec((tm, BRANCH_W), tok),
            pl.BlockSpec((tm, BRANCH_W), tok),
            pl.BlockSpec((tm, BRANCH_W), tok),
            pl.BlockSpec((1, D_MODEL), const2),
            pl.BlockSpec((1, D_MODEL), const2),
            pl.BlockSpec((D_MODEL, N_BRANCH * D_MODEL), const2, pipeline_mode=pl.Buffered(1)),
            pl.BlockSpec((N_BRANCH, BRANCH_W, D_MODEL), lambda i: (0, 0, 0), pipeline_mode=pl.Buffered(1)),
            pl.BlockSpec((D_MODEL, D_MODEL), const2, pipeline_mode=pl.Buffered(1)),
        ],
        out_specs=pl.BlockSpec((tm, D_MODEL), tok),
        out_shape=jax.ShapeDtypeStruct((rows, D_MODEL), F32),
        compiler_params=_params("arbitrary"),
        name="merge",
    )(h2, ya, yb, yc, g_in, g_out, w_gt, w_br, w_o)


def _rot_cols(w):
    d, n = w.shape
    w = w.reshape(d, n // HEAD_DIM, 2, HEAD_DIM // 2)
    return jnp.stack([-w[:, :, 1], w[:, :, 0]], axis=2).reshape(d, n)


def _pad_heads(w):
    d, n = w.shape
    w = w.reshape(d, n // HEAD_DIM, HEAD_DIM)
    return jnp.pad(w, ((0, 0), (0, 0), (0, LANE - HEAD_DIM))).reshape(d, -1)


def _pack_w_in(w_in):
    sizes = (BRANCH_W, BRANCH_W, BRANCH_W, ATT_HEADS * HEAD_DIM, ATT_KV_HEADS * HEAD_DIM,
             ATT_KV_HEADS * HEAD_DIM, IDX_HEADS * HEAD_DIM, HEAD_DIM, IDX_HEADS,
             BRANCH_W, BRANCH_W, BRANCH_W, BRANCH_W, ML_HEADS, ML_HEADS, N_BRANCH * D_MODEL)
    offs = np.cumsum(sizes)[:-1].tolist()
    (cx, cb, cc, q, k, v, qi, ki, wi, mq, mk, mv, mo, mi, mf, gt) = jnp.split(w_in, offs, axis=1)
    d = w_in.shape[0]
    zeros64 = jnp.zeros((d, HEAD_DIM), w_in.dtype)
    v_parts = []
    for g in range(ATT_KV_HEADS):
        vg = v[:, g * HEAD_DIM:(g + 1) * HEAD_DIM]
        v_parts += [vg, zeros64, zeros64, vg]
    small = jnp.concatenate([wi, mi, mf, jnp.zeros((d, SEG_SM - 3 * IDX_HEADS), w_in.dtype)], axis=1)
    packed = jnp.concatenate([
        cx, cb, cc,
        _pad_heads(q), _pad_heads(_rot_cols(q)),
        _pad_heads(k), _pad_heads(_rot_cols(k)),
        *v_parts,
        _pad_heads(qi), _pad_heads(_rot_cols(qi)),
        _pad_heads(ki), _pad_heads(_rot_cols(ki)),
        small, mq, mk, mv, mo], axis=1)
    assert packed.shape[1] == W_PACKED
    return packed.astype(BF16), gt.astype(BF16)


def _rope_tables(tp):
    inv = 1.0 / (ROPE_THETA ** (jnp.arange(0, HEAD_DIM, 2, dtype=F32) / HEAD_DIM))
    ang = jnp.arange(tp, dtype=F32)[:, None] * inv[None, :]
    reps = 2 * LANE // HEAD_DIM
    return jnp.tile(jnp.cos(ang), (1, reps)), jnp.tile(jnp.sin(ang), (1, reps))


def kernel(x, meta, norm_g, w_ffn_gu, w_ffn_down, w_in, w_conv, b_igate, b_fgate, mh_norm_g, w_branch, w_out):
    B, S, _ = x.shape
    depth = norm_g.shape[0]
    T = N_META + S
    TP = -(-T // SEQ_ALIGN) * SEQ_ALIGN
    topk = min(IDX_TOPK_MAX, S // 4)

    h = jnp.concatenate([jnp.broadcast_to(meta.astype(x.dtype)[None], (B, N_META, D_MODEL)), x,
                         jnp.zeros((B, TP - T, D_MODEL), x.dtype)], axis=1)
    cos, sin = _rope_tables(TP)
    vone = np.zeros((1, SEG_V), np.float32)
    for g in range(ATT_KV_HEADS):
        vone[0, (2 * g) * LANE + HEAD_DIM] = 1.0
        vone[0, (2 * g + 1) * LANE] = 1.0
    vone = jnp.asarray(vone)

    h2 = h.reshape(B * TP, D_MODEL)
    for l in range(depth):
        g = norm_g[l][:, None, :]
        w_packed, w_gt = _pack_w_in(w_in[l])
        smb = jnp.concatenate([jnp.zeros((IDX_HEADS,), F32), b_igate[l], b_fgate[l],
                               jnp.zeros((SEG_SM - 3 * IDX_HEADS,), F32)])[None, :]
        wconv = jnp.pad(w_conv[l], ((0, 8 - CONV_K), (0, 0)))

        h2 = _ffn(h2, g[0], g[1], w_ffn_gu[l, 0].astype(BF16), w_ffn_down[l, 0].astype(BF16))
        (ya, q, k, v, qi, ki, sm, mq, mk, mv, mo) = _inproj(
            h2.reshape(B, TP, D_MODEL), g[2], cos, sin, wconv, smb, vone, w_packed)
        yb = _attention(q, qi, sm, k, v, ki, topk)
        yc = _mlstm(mq, mk, mv, mo, sm, mh_norm_g[l][None, :])
        flat = lambda a: a.reshape(B * TP, BRANCH_W)
        h2 = _merge(h2, flat(ya), flat(yb), flat(yc), g[2], g[3], w_gt,
                    w_branch[l].astype(BF16), w_out[l].astype(BF16))
        h2 = _ffn(h2, g[4], g[5], w_ffn_gu[l, 1].astype(BF16), w_ffn_down[l, 1].astype(BF16))
    return h2.reshape(B, TP, D_MODEL)[:, N_META:T]
```

```python
import functools

import numpy as np
import jax
import jax.numpy as jnp
from jax import lax
from jax.experimental import pallas as pl
from jax.experimental.pallas import tpu as pltpu

F32 = jnp.float32
BF16 = jnp.bfloat16

D_MODEL = 1024
N_META = 16
EPS = 1e-6
D_FF = 11 * D_MODEL // 4
BRANCH_W = D_MODEL // 2
N_BRANCH = 3
CONV_K = 3
HEAD_DIM = 64
ATT_HEADS = 8
ATT_KV_HEADS = 2
ATT_REP = ATT_HEADS // ATT_KV_HEADS
IDX_HEADS = 4
IDX_TOPK_MAX = 256
ML_HEADS = 4
ML_DIM = 128
ROPE_THETA = 10000.0

LANE = 128
FF_CHUNK = 256
SEQ_ALIGN = LANE
VMEM_LIMIT = 56 * 1024 * 1024
Q_SCALE = HEAD_DIM ** -0.5 * 1.4426950408889634
INT_MIN = -2 ** 31
NEG_BIG = -1e30

SEG_CONV = 3 * BRANCH_W
SEG_Q = ATT_HEADS * LANE
SEG_K = ATT_KV_HEADS * LANE
SEG_V = 2 * ATT_KV_HEADS * LANE
SEG_QI = IDX_HEADS * LANE
SEG_KI = LANE
SEG_SM = LANE
SEG_ML = 4 * BRANCH_W
OFF_CONV = 0
OFF_Q = OFF_CONV + SEG_CONV
OFF_QR = OFF_Q + SEG_Q
OFF_K = OFF_QR + SEG_Q
OFF_KR = OFF_K + SEG_K
OFF_V = OFF_KR + SEG_K
OFF_QI = OFF_V + SEG_V
OFF_QIR = OFF_QI + SEG_QI
OFF_KI = OFF_QIR + SEG_QI
OFF_KIR = OFF_KI + SEG_KI
OFF_SM = OFF_KIR + SEG_KI
OFF_ML = OFF_SM + SEG_SM
W_PACKED = OFF_ML + SEG_ML


def _pick_tile(n, candidates):
    for c in candidates:
        if n % c == 0:
            return c
    raise ValueError(f"no tile for {n}")


def _rms(x, g):
    return x * lax.rsqrt(jnp.mean(x * x, axis=-1, keepdims=True) + EPS) * g


def _dot(a, b):
    return jnp.dot(a, b, preferred_element_type=F32)


def _dot_nt(a, b):
    return lax.dot_general(a, b, (((1,), (1,)), ((), ())), preferred_element_type=F32)


def _params(*sem):
    return pltpu.CompilerParams(dimension_semantics=sem, vmem_limit_bytes=VMEM_LIMIT)


def _ffn_kernel(h_ref, gin_ref, gout_ref, wgu_ref, wd_ref, o_ref, act_ref):
    x = h_ref[...]
    xn = _rms(x, gin_ref[...]).astype(BF16)
    for c in range(D_FF // FF_CHUNK):
        lo = c * FF_CHUNK
        gate = _dot(xn, wgu_ref[:, lo:lo + FF_CHUNK])
        up = _dot(xn, wgu_ref[:, D_FF + lo:D_FF + lo + FF_CHUNK])
        act_ref[:, lo:lo + FF_CHUNK] = (gate * jax.nn.sigmoid(gate) * up).astype(BF16)
    y = _dot(act_ref[...], wd_ref[...])
    o_ref[...] = x + 0.5 * _rms(y, gout_ref[...])


def _ffn(h2, g_in, g_out, w_gu, w_down):
    rows = h2.shape[0]
    tm = _pick_tile(rows, (512, 384, 256, 128))
    const = lambda i: (0, 0)
    return pl.pallas_call(
        _ffn_kernel,
        grid=(rows // tm,),
        in_specs=[
            pl.BlockSpec((tm, D_MODEL), lambda i: (i, 0)),
            pl.BlockSpec((1, D_MODEL), const),
            pl.BlockSpec((1, D_MODEL), const),
            pl.BlockSpec((D_MODEL, 2 * D_FF), const, pipeline_mode=pl.Buffered(1)),
            pl.BlockSpec((D_FF, D_MODEL), const, pipeline_mode=pl.Buffered(1)),
        ],
        out_specs=pl.BlockSpec((tm, D_MODEL), lambda i: (i, 0)),
        out_shape=jax.ShapeDtypeStruct((rows, D_MODEL), F32),
        scratch_shapes=[pltpu.VMEM((tm, D_FF), BF16)],
        compiler_params=_params("arbitrary"),
        name="ffn",
    )(h2, g_in, g_out, w_gu, w_down)


def _inproj_kernel(h_ref, g_ref, cos_ref, sin_ref, wconv_ref, smb_ref, vone_ref, w_ref,
                   ya_ref, q_ref, k_ref, v_ref, qi_ref, ki_ref, sm_ref,
                   mq_ref, mk_ref, mv_ref, mo_ref, carry_ref):
    t = pl.program_id(1)
    tm = h_ref.shape[1]

    @pl.when(t == 0)
    def _():
        carry_ref[...] = jnp.zeros_like(carry_ref)

    xn = _rms(h_ref[0], g_ref[...]).astype(BF16)

    def proj(off, width):
        return _dot(xn, w_ref[:, off:off + width])

    cx = proj(OFF_CONV, BRANCH_W)
    cc = proj(OFF_CONV + 2 * BRANCH_W, BRANCH_W)
    z = cc * cx
    row = lax.broadcasted_iota(jnp.int32, z.shape, 0)
    prev1 = jnp.broadcast_to(carry_ref[7:8, :], z.shape)
    prev2 = jnp.broadcast_to(carry_ref[6:7, :], z.shape)
    z1 = jnp.where(row == 0, prev1, pltpu.roll(z, 1, 0))
    z2 = jnp.where(row == 0, prev2, jnp.where(row == 1, prev1, pltpu.roll(z, 2, 0)))
    carry_ref[...] = z[tm - 8:tm, :]
    conv = wconv_ref[0:1, :] * z2 + wconv_ref[1:2, :] * z1 + wconv_ref[2:3, :] * z
    cb = proj(OFF_CONV + BRANCH_W, BRANCH_W)
    ya_ref[0] = (cb * conv).astype(BF16)

    cos = cos_ref[...]
    sin = sin_ref[...]

    def rope_to(out_ref, off, off_rot, width, scale, transposed):
        step = min(width, FF_CHUNK)
        for c0 in range(0, width, step):
            x2 = proj(off + c0, step)
            xr2 = proj(off_rot + c0, step)
            for c in range(c0 // LANE, (c0 + step) // LANE):
                lanes = slice(c * LANE - c0, (c + 1) * LANE - c0)
                y = x2[:, lanes] * cos + xr2[:, lanes] * sin
                if scale != 1.0:
                    y = y * scale
                if transposed:
                    out_ref[0, c * LANE:(c + 1) * LANE, :] = y.T.astype(BF16)
                else:
                    out_ref[0, :, c * LANE:(c + 1) * LANE] = y.astype(BF16)

    rope_to(q_ref, OFF_Q, OFF_QR, SEG_Q, Q_SCALE, False)
    rope_to(k_ref, OFF_K, OFF_KR, SEG_K, 1.0, True)
    rope_to(qi_ref, OFF_QI, OFF_QIR, SEG_QI, 1.0, False)
    rope_to(ki_ref, OFF_KI, OFF_KIR, SEG_KI, 1.0, True)
    v_ref[0] = (proj(OFF_V, SEG_V) + vone_ref[...]).astype(BF16)

    raw = proj(OFF_SM, SEG_SM) + smb_ref[...]
    col = lax.broadcasted_iota(jnp.int32, raw.shape, 1)
    logsig = jnp.minimum(raw, 0.0) - jnp.log(1.0 + jnp.exp(-jnp.abs(raw)))
    idx_scale = (IDX_HEADS ** -0.5) * (HEAD_DIM ** -0.5)
    sm_ref[0] = jnp.where(col < IDX_HEADS, raw * idx_scale,
                          jnp.where(col < 2 * IDX_HEADS, raw,
                                    jnp.where(col < 3 * IDX_HEADS, logsig, 0.0)))

    mq_ref[0] = proj(OFF_ML, BRANCH_W).astype(BF16)
    mk_ref[0] = (proj(OFF_ML + BRANCH_W, BRANCH_W) * (ML_DIM ** -0.5)).astype(BF16)
    mv_ref[0] = proj(OFF_ML + 2 * BRANCH_W, BRANCH_W).astype(BF16)
    mo_ref[0] = proj(OFF_ML + 3 * BRANCH_W, BRANCH_W)


def _inproj(h3, g, cos, sin, wconv, smb, vone, w_packed):
    B, TP, _ = h3.shape
    tm = _pick_tile(TP, (384, 256, 128))
    const = lambda b, t: (0, 0)
    tok = lambda b, t: (b, t, 0)

    def out(width, dtype):
        return (pl.BlockSpec((1, tm, width), tok), jax.ShapeDtypeStruct((B, TP, width), dtype))

    def out_t(width, dtype):
        return (pl.BlockSpec((1, width, tm), lambda b, t: (b, 0, t)),
                jax.ShapeDtypeStruct((B, width, TP), dtype))

    outs = [out(BRANCH_W, BF16), out(SEG_Q, BF16), out_t(SEG_K, BF16), out(SEG_V, BF16),
            out(SEG_QI, BF16), out_t(SEG_KI, BF16), out(SEG_SM, F32),
            out(BRANCH_W, BF16), out(BRANCH_W, BF16), out(BRANCH_W, BF16), out(BRANCH_W, F32)]
    return pl.pallas_call(
        _inproj_kernel,
        grid=(B, TP // tm),
        in_specs=[
            pl.BlockSpec((1, tm, D_MODEL), tok),
            pl.BlockSpec((1, D_MODEL), const),
            pl.BlockSpec((tm, LANE), lambda b, t: (t, 0)),
            pl.BlockSpec((tm, LANE), lambda b, t: (t, 0)),
            pl.BlockSpec((8, BRANCH_W), const),
            pl.BlockSpec((1, SEG_SM), const),
            pl.BlockSpec((1, SEG_V), const),
            pl.BlockSpec((D_MODEL, W_PACKED), const, pipeline_mode=pl.Buffered(1)),
        ],
        out_specs=[o[0] for o in outs],
        out_shape=[o[1] for o in outs],
        scratch_shapes=[pltpu.VMEM((8, BRANCH_W), F32)],
        compiler_params=_params("arbitrary", "arbitrary"),
        name="inproj",
    )(h3, g, cos, sin, wconv, smb, vone, w_packed)


def _score_key(sc):
    bits = lax.bitcast_convert_type(sc, jnp.int32)
    return jnp.where(bits < 0, INT_MIN - bits, bits)


def _attn_kernel(q_ref, qi_ref, sm_ref, k_ref, v_ref, ki_ref, o_ref,
                 keys_ref, wb_ref, acc_ref, m_ref, thr_ref, q4_ref, qi4_ref, *, topk, ub):
    i = pl.program_id(1)
    blk = LANE
    uw = ub * blk
    nunits = i + 1
    row = lax.broadcasted_iota(jnp.int32, (uw, blk), 0)
    col = lax.broadcasted_iota(jnp.int32, (uw, blk), 1)

    def unit(u):
        return pl.ds(pl.multiple_of(u * uw, blk), uw)

    def block(u, cb):
        return pl.ds(pl.multiple_of(u * uw + cb * blk, blk), blk)

    for h in range(IDX_HEADS):
        wb_ref[h] = jnp.broadcast_to(sm_ref[0, :, h:h + 1], (uw, blk))
        qi4_ref[h * uw:(h + 1) * uw, :] = qi_ref[0, :, h * LANE:(h + 1) * LANE]
    qpos = i * uw + row

    def p1(u, c):
        rel = _dot(qi4_ref[...], ki_ref[0, :, unit(u)])
        for cb in range(ub):
            lanes = slice(cb * blk, (cb + 1) * blk)
            sc = wb_ref[0] * jnp.maximum(rel[0:uw, lanes], 0.0)
            for h in range(1, IDX_HEADS):
                sc = sc + wb_ref[h] * jnp.maximum(rel[h * uw:(h + 1) * uw, lanes], 0.0)
            kpos = u * uw + cb * blk + col
            keys_ref[:, block(u, cb)] = jnp.where(kpos <= qpos, _score_key(sc), INT_MIN)
        return c

    lax.fori_loop(0, nunits, p1, 0)

    rows_of = [slice(rb * blk, (rb + 1) * blk) for rb in range(ub)]
    col1 = lax.broadcasted_iota(jnp.int32, (blk, blk), 1)

    def lane_counts(rb, pred):
        def body(u, acc):
            for cb in range(ub):
                hit = pred(keys_ref[rows_of[rb], block(u, cb)], u * uw + cb * blk)
                acc = acc + jnp.where(hit, 1.0, 0.0)
            return acc
        return lax.fori_loop(0, nunits, body, jnp.zeros((blk, blk), F32))

    def count_where(rb, pred):
        return jnp.sum(lane_counts(rb, pred), axis=-1, keepdims=True)

    def ge(cand):
        cand_b = jnp.broadcast_to(cand, (blk, blk))
        return lambda kb, base: kb >= cand_b

    def count_ge(rb, cand):
        return count_where(rb, ge(cand))

    kf = float(topk)
    zero = jnp.zeros((blk, 1), jnp.int32)
    c0 = tuple(count_ge(rb, zero) for rb in range(ub))
    r0 = tuple(jnp.where(c0[rb] >= kf, zero, INT_MIN) for rb in range(ub))

    def bit_step(t, carry):
        rs, cs = carry
        bit = jnp.left_shift(jnp.int32(1), 30 - t)
        cands = [rs[rb] | bit for rb in range(ub)]
        accs = [lane_counts(rb, ge(cands[rb])) for rb in range(ub)]
        cnts = [jnp.sum(accs[rb], axis=-1, keepdims=True) for rb in range(ub)]
        return (tuple(jnp.where(cnts[rb] >= kf, cands[rb], rs[rb]) for rb in range(ub)),
                tuple(jnp.where(cnts[rb] >= kf, cnts[rb], cs[rb]) for rb in range(ub)))

    rs, cs = lax.fori_loop(0, 31, bit_step, (r0, c0))
    ties = [jnp.logical_and(rs[rb] > INT_MIN, cs[rb] > kf) for rb in range(ub)]
    any_tie = jnp.max(jnp.where(ties[0], 1.0, 0.0))
    for rb in range(1, ub):
        any_tie = jnp.maximum(any_tie, jnp.max(jnp.where(ties[rb], 1.0, 0.0)))

    @pl.when(any_tie > 0.0)
    def _():
        for rb in range(ub):
            r_b = jnp.broadcast_to(rs[rb], (blk, blk))
            tie_b = jnp.broadcast_to(ties[rb], (blk, blk))
            need = kf - count_ge(rb, rs[rb] + 1)

            def pos_step(t, pos):
                cand = pos + jnp.left_shift(jnp.int32(1), 12 - t)
                cand_b = jnp.broadcast_to(cand, (blk, blk))
                cnt = count_where(rb, lambda kb, base: jnp.logical_and(kb == r_b, col1 + base < cand_b))
                return jnp.where(cnt < need, cand, pos)

            pos_b = jnp.broadcast_to(lax.fori_loop(0, 13, pos_step, zero), (blk, blk))

            def fix(u, c):
                for cb in range(ub):
                    kb = keys_ref[rows_of[rb], block(u, cb)]
                    late = col1 + (u * uw + cb * blk) > pos_b
                    drop = jnp.logical_and(jnp.logical_and(kb == r_b, late), tie_b)
                    keys_ref[rows_of[rb], block(u, cb)] = jnp.where(drop, r_b - 1, kb)
                return c

            lax.fori_loop(0, nunits, fix, 0)

    for rb in range(ub):
        thr_ref[rows_of[rb], :] = jnp.broadcast_to(jnp.maximum(rs[rb], INT_MIN + 1), (blk, blk))
    thr_b = thr_ref[...]

    m_ref[...] = jnp.full(m_ref.shape, NEG_BIG, F32)
    acc_ref[...] = jnp.zeros(acc_ref.shape, F32)
    for g in range(ATT_KV_HEADS):
        for rr in range(ATT_REP):
            h = g * ATT_REP + rr
            q4_ref[g, rr * uw:(rr + 1) * uw, :] = q_ref[0, :, h * LANE:(h + 1) * LANE]

    def p3(u, c):
        bias = [jnp.where(keys_ref[:, block(u, cb)] >= thr_b, 0.0, NEG_BIG) for cb in range(ub)]
        for g in range(ATT_KV_HEADS):
            s4 = _dot(q4_ref[g], k_ref[0, g * LANE:(g + 1) * LANE, unit(u)])
            ps, alphas = [], []
            for rr in range(ATT_REP):
                h = g * ATT_REP + rr
                s = [s4[rr * uw:(rr + 1) * uw, cb * blk:(cb + 1) * blk] + bias[cb] for cb in range(ub)]
                smax = s[0]
                for cb in range(1, ub):
                    smax = jnp.maximum(smax, s[cb])
                m_old = m_ref[h]
                m_new = jnp.maximum(m_old, jnp.max(smax, axis=-1, keepdims=True))
                ps.append(jnp.concatenate([jnp.exp2(sc - m_new).astype(BF16) for sc in s], axis=1))
                alphas.append(jnp.exp2(m_old - m_new))
                m_ref[h] = m_new
            for par in range(2):
                pv = _dot(jnp.concatenate([ps[par], ps[par + 2]], axis=0),
                          v_ref[0, unit(u), (2 * g + par) * LANE:(2 * g + par + 1) * LANE])
                for n, rr in enumerate((par, par + 2)):
                    h = g * ATT_REP + rr
                    acc_ref[h] = alphas[rr] * acc_ref[h] + pv[n * uw:(n + 1) * uw]
        return c

    lax.fori_loop(0, nunits, p3, 0)

    for pr in range(ATT_HEADS // 2):
        ae = acc_ref[2 * pr]
        ao = acc_ref[2 * pr + 1]
        oe = ae / ae[:, HEAD_DIM:HEAD_DIM + 1]
        oo = ao / ao[:, 0:1]
        o_ref[0, :, pr * LANE:(pr + 1) * LANE] = jnp.where(col < HEAD_DIM, oe, oo).astype(BF16)


def _attention(q, qi, sm, k, v, ki, topk):
    B, TP, _ = q.shape
    nblk = TP // LANE
    ub = _pick_tile(nblk, (3, 2, 1))
    uw = ub * LANE
    assert TP <= 2 ** 13
    qmap = lambda b, i: (b, i, 0)
    kmap = lambda b, i: (b, 0, 0)
    return pl.pallas_call(
        functools.partial(_attn_kernel, topk=topk, ub=ub),
        grid=(B, TP // uw),
        in_specs=[
            pl.BlockSpec((1, uw, SEG_Q), qmap),
            pl.BlockSpec((1, uw, SEG_QI), qmap),
            pl.BlockSpec((1, uw, SEG_SM), qmap),
            pl.BlockSpec((1, SEG_K, TP), kmap),
            pl.BlockSpec((1, TP, SEG_V), kmap),
            pl.BlockSpec((1, SEG_KI, TP), kmap),
        ],
        out_specs=pl.BlockSpec((1, uw, BRANCH_W), qmap),
        out_shape=jax.ShapeDtypeStruct((B, TP, BRANCH_W), BF16),
        scratch_shapes=[
            pltpu.VMEM((uw, TP), jnp.int32),
            pltpu.VMEM((IDX_HEADS, uw, LANE), F32),
            pltpu.VMEM((ATT_HEADS, uw, LANE), F32),
            pltpu.VMEM((ATT_HEADS, uw, LANE), F32),
            pltpu.VMEM((uw, LANE), jnp.int32),
            pltpu.VMEM((ATT_KV_HEADS, ATT_REP * uw, LANE), BF16),
            pltpu.VMEM((IDX_HEADS * uw, LANE), BF16),
        ],
        compiler_params=_params("arbitrary", "arbitrary"),
        name="attn",
    )(q, qi, sm, k, v, ki)


def _mlstm_kernel(q_ref, k_ref, v_ref, mo_ref, sm_ref, g_ref, o_ref, c_ref, m_ref):
    L = LANE

    @pl.when(pl.program_id(1) == 0)
    def _():
        c_ref[...] = jnp.zeros_like(c_ref)
        m_ref[...] = jnp.zeros_like(m_ref)

    row = lax.broadcasted_iota(jnp.int32, (L, L), 0)
    col = lax.broadcasted_iota(jnp.int32, (L, L), 1)
    tril = col <= row
    ones_col = jnp.where(col == 0, 1.0, 0.0).astype(BF16)
    for sc in range(q_ref.shape[1] // L):
        _mlstm_chunk(slice(sc * L, (sc + 1) * L), tril, ones_col,
                     q_ref, k_ref, v_ref, mo_ref, sm_ref, g_ref, o_ref, c_ref, m_ref)


def _mlstm_chunk(rows, tril, ones_col, q_ref, k_ref, v_ref, mo_ref, sm_ref, g_ref, o_ref, c_ref, m_ref):
    L = LANE
    sm = sm_ref[0, rows, :]
    cum = jnp.dot(jnp.where(tril, 1.0, 0.0), sm, precision=lax.Precision.HIGHEST,
                  preferred_element_type=F32)
    sm_t = sm.T
    cum_t = cum.T

    for h in range(ML_HEADS):
        lanes = slice(h * ML_DIM, (h + 1) * ML_DIM)
        ci, cf = IDX_HEADS + h, 2 * IDX_HEADS + h
        b_col = cum[:, cf:cf + 1]
        b_row = cum_t[cf:cf + 1, :]
        b_last = cum[L - 1:L, cf:cf + 1]
        i_col = sm[:, ci:ci + 1]
        i_row = sm_t[ci:ci + 1, :]
        m_old = m_ref[h][0:1, 0:1]
        qh = q_ref[0, rows, lanes]
        kh = k_ref[0, rows, lanes]
        v_aug = jnp.concatenate([v_ref[0, rows, lanes], ones_col], axis=1)

        dm = jnp.where(tril, b_col - b_row + i_row, -jnp.inf)
        a = b_col + m_old
        mq = jnp.maximum(a, jnp.max(dm, axis=-1, keepdims=True))
        inter = jnp.exp(a - mq)
        wqk = jnp.exp(dm - mq) * _dot_nt(qh, kh)
        c_old = c_ref[h]
        res = inter * _dot(qh, c_old.astype(BF16)) + _dot(wqk.astype(BF16), v_aug)
        den = res[:, ML_DIM:ML_DIM + 1]
        hc = res[:, 0:ML_DIM] / jnp.maximum(jnp.abs(den), jnp.exp(-mq))

        g_col = b_last - b_col + i_col
        m_new = jnp.maximum(b_last + m_old, jnp.max(g_col, axis=0, keepdims=True))
        decay = jnp.exp(b_last + m_old - m_new)
        wk = jnp.exp(g_col - m_new) * kh.astype(F32)
        c_ref[h] = decay * c_old + _dot(wk.T.astype(BF16), v_aug)
        m_ref[h] = jnp.broadcast_to(m_new, m_ref.shape[1:])

        hn = hc * lax.rsqrt(jnp.mean(hc * hc, axis=-1, keepdims=True) + EPS) * g_ref[:, lanes]
        o_ref[0, rows, lanes] = (hn * jax.nn.sigmoid(mo_ref[0, rows, lanes])).astype(BF16)


def _mlstm(mq, mk, mv, mo, sm, g):
    B, TP, _ = mq.shape
    tm = _pick_tile(TP, (384, 256, 128))
    tok = lambda b, c: (b, c, 0)
    return pl.pallas_call(
        _mlstm_kernel,
        grid=(B, TP // tm),
        in_specs=[
            pl.BlockSpec((1, tm, BRANCH_W), tok),
            pl.BlockSpec((1, tm, BRANCH_W), tok),
            pl.BlockSpec((1, tm, BRANCH_W), tok),
            pl.BlockSpec((1, tm, BRANCH_W), tok),
            pl.BlockSpec((1, tm, SEG_SM), tok),
            pl.BlockSpec((1, BRANCH_W), lambda b, c: (0, 0)),
        ],
        out_specs=pl.BlockSpec((1, tm, BRANCH_W), tok),
        out_shape=jax.ShapeDtypeStruct((B, TP, BRANCH_W), BF16),
        scratch_shapes=[
            pltpu.VMEM((ML_HEADS, ML_DIM, 2 * ML_DIM), F32),
            pltpu.VMEM((ML_HEADS, 8, LANE), F32),
        ],
        compiler_params=_params("arbitrary", "arbitrary"),
        name="mlstm",
    )(mq, mk, mv, mo, sm, g)


def _merge_kernel(h_ref, ya_ref, yb_ref, yc_ref, gin_ref, gout_ref, wgt_ref, wbr_ref, wo_ref, o_ref):
    x = h_ref[...]
    xn = _rms(x, gin_ref[...]).astype(BF16)
    merged = None
    for n, y_ref in enumerate((ya_ref, yb_ref, yc_ref)):
        gate = jax.nn.sigmoid(_dot(xn, wgt_ref[:, n * D_MODEL:(n + 1) * D_MODEL]))
        term = gate * _dot(y_ref[...], wbr_ref[n])
        merged = term if merged is None else merged + term
    y = _dot(merged.astype(BF16), wo_ref[...])
    o_ref[...] = x + _rms(y, gout_ref[...])


def _merge(h2, ya, yb, yc, g_in, g_out, w_gt, w_br, w_o):
    rows = h2.shape[0]
    tm = _pick_tile(rows, (512, 384, 256, 128))
    const2 = lambda i: (0, 0)
    tok = lambda i: (i, 0)
    return pl.pallas_call(
        _merge_kernel,
        grid=(rows // tm,),
        in_specs=[
            pl.BlockSpec((tm, D_MODEL), tok),
            pl.BlockSpec((tm, BRANCH_W), tok),
            pl.BlockSpec((tm, BRANCH_W), tok),
            pl.BlockSpec((tm, BRANCH_W), tok),
            pl.BlockSpec((1, D_MODEL), const2),
            pl.BlockSpec((1, D_MODEL), const2),
            pl.BlockSpec((D_MODEL, N_BRANCH * D_MODEL), const2, pipeline_mode=pl.Buffered(1)),
            pl.BlockSpec((N_BRANCH, BRANCH_W, D_MODEL), lambda i: (0, 0, 0), pipeline_mode=pl.Buffered(1)),
            pl.BlockSpec((D_MODEL, D_MODEL), const2, pipeline_mode=pl.Buffered(1)),
        ],
        out_specs=pl.BlockSpec((tm, D_MODEL), tok),
        out_shape=jax.ShapeDtypeStruct((rows, D_MODEL), F32),
        compiler_params=_params("arbitrary"),
        name="merge",
    )(h2, ya, yb, yc, g_in, g_out, w_gt, w_br, w_o)


def _rot_cols(w):
    d, n = w.shape
    w = w.reshape(d, n // HEAD_DIM, 2, HEAD_DIM // 2)
    return jnp.stack([-w[:, :, 1], w[:, :, 0]], axis=2).reshape(d, n)


def _pad_heads(w):
    d, n = w.shape
    w = w.reshape(d, n // HEAD_DIM, HEAD_DIM)
    return jnp.pad(w, ((0, 0), (0, 0), (0, LANE - HEAD_DIM))).reshape(d, -1)


def _pack_w_in(w_in):
    sizes = (BRANCH_W, BRANCH_W, BRANCH_W, ATT_HEADS * HEAD_DIM, ATT_KV_HEADS * HEAD_DIM,
             ATT_KV_HEADS * HEAD_DIM, IDX_HEADS * HEAD_DIM, HEAD_DIM, IDX_HEADS,
             BRANCH_W, BRANCH_W, BRANCH_W, BRANCH_W, ML_HEADS, ML_HEADS, N_BRANCH * D_MODEL)
    offs = np.cumsum(sizes)[:-1].tolist()
    (cx, cb, cc, q, k, v, qi, ki, wi, mq, mk, mv, mo, mi, mf, gt) = jnp.split(w_in, offs, axis=1)
    d = w_in.shape[0]
    zeros64 = jnp.zeros((d, HEAD_DIM), w_in.dtype)
    v_parts = []
    for g in range(ATT_KV_HEADS):
        vg = v[:, g * HEAD_DIM:(g + 1) * HEAD_DIM]
        v_parts += [vg, zeros64, zeros64, vg]
    small = jnp.concatenate([wi, mi, mf, jnp.zeros((d, SEG_SM - 3 * IDX_HEADS), w_in.dtype)], axis=1)
    packed = jnp.concatenate([
        cx, cb, cc,
        _pad_heads(q), _pad_heads(_rot_cols(q)),
        _pad_heads(k), _pad_heads(_rot_cols(k)),
        *v_parts,
        _pad_heads(qi), _pad_heads(_rot_cols(qi)),
        _pad_heads(ki), _pad_heads(_rot_cols(ki)),
        small, mq, mk, mv, mo], axis=1)
    assert packed.shape[1] == W_PACKED
    return packed.astype(BF16), gt.astype(BF16)


def _rope_tables(tp):
    inv = 1.0 / (ROPE_THETA ** (jnp.arange(0, HEAD_DIM, 2, dtype=F32) / HEAD_DIM))
    ang = jnp.arange(tp, dtype=F32)[:, None] * inv[None, :]
    reps = 2 * LANE // HEAD_DIM
    return jnp.tile(jnp.cos(ang), (1, reps)), jnp.tile(jnp.sin(ang), (1, reps))


def kernel(x, meta, norm_g, w_ffn_gu, w_ffn_down, w_in, w_conv, b_igate, b_fgate, mh_norm_g, w_branch, w_out):
    B, S, _ = x.shape
    depth = norm_g.shape[0]
    T = N_META + S
    TP = -(-T // SEQ_ALIGN) * SEQ_ALIGN
    topk = min(IDX_TOPK_MAX, S // 4)

    h = jnp.concatenate([jnp.broadcast_to(meta.astype(x.dtype)[None], (B, N_META, D_MODEL)), x,
                         jnp.zeros((B, TP - T, D_MODEL), x.dtype)], axis=1)
    cos, sin = _rope_tables(TP)
    vone = np.zeros((1, SEG_V), np.float32)
    for g in range(ATT_KV_HEADS):
        vone[0, (2 * g) * LANE + HEAD_DIM] = 1.0
        vone[0, (2 * g + 1) * LANE] = 1.0
    vone = jnp.asarray(vone)

    h2 = h.reshape(B * TP, D_MODEL)
    for l in range(depth):
        g = norm_g[l][:, None, :]
        w_packed, w_gt = _pack_w_in(w_in[l])
        smb = jnp.concatenate([jnp.zeros((IDX_HEADS,), F32), b_igate[l], b_fgate[l],
                               jnp.zeros((SEG_SM - 3 * IDX_HEADS,), F32)])[None, :]
        wconv = jnp.pad(w_conv[l], ((0, 8 - CONV_K), (0, 0)))

        h2 = _ffn(h2, g[0], g[1], w_ffn_gu[l, 0].astype(BF16), w_ffn_down[l, 0].astype(BF16))
        (ya, q, k, v, qi, ki, sm, mq, mk, mv, mo) = _inproj(
            h2.reshape(B, TP, D_MODEL), g[2], cos, sin, wconv, smb, vone, w_packed)
        yb = _attention(q, qi, sm, k, v, ki, topk)
        yc = _mlstm(mq, mk, mv, mo, sm, mh_norm_g[l][None, :])
        flat = lambda a: a.reshape(B * TP, BRANCH_W)
        h2 = _merge(h2, flat(ya), flat(yb), flat(yc), g[2], g[3], w_gt,
                    w_branch[l].astype(BF16), w_out[l].astype(BF16))
        h2 = _ffn(h2, g[4], g[5], w_ffn_gu[l, 1].astype(BF16), w_ffn_down[l, 1].astype(BF16))
    return h2.reshape(B, TP, D_MODEL)[:, N_META:T]
```
